```python
import math
import jax, jax.numpy as jnp
from jax import lax
import numpy as np

D_MODEL = 2048
BATCH = 4
SEQ = 4096
DEPTH = 2
DEC_BATCH = 128
DEC_SEQ = 1
PAST_LEN = 16384
PAGE_SIZE = 128

N_RET_LAYERS = (DEPTH + 1) // 2
N_MLA_LAYERS = DEPTH // 2
RET_DK = 256
RET_HEADS = D_MODEL // RET_DK
RET_DV = 2 * RET_DK
RET_QK_W = RET_HEADS * RET_DK
RET_V_W = RET_HEADS * RET_DV
RET_CHUNK = 128
MLA_HEADS = 16
Q_LORA = 512
KV_LORA = 512
NOPE_DIM = 128
ROPE_DIM = 64
V_DIM = 128
MLA_IN_W = Q_LORA + KV_LORA + ROPE_DIM
Q_BLOCK = 128
D_FF = 4 * D_MODEL
ROPE_BASE = 10000.0
NORM_EPS = 1e-6
GN_EPS = 1e-5
NEG = -1e30

kernel_name = "retnet_mla_hybrid_step"


def rmsnorm(x, w):
    x32 = x.astype(jnp.float32)
    y = x32 * lax.rsqrt(jnp.mean(x32 * x32, axis=-1, keepdims=True) + NORM_EPS) * w.astype(jnp.float32)
    return y.astype(x.dtype)


def rope(x, pos):
    d2 = x.shape[-1] // 2
    inv = ROPE_BASE ** (-jnp.arange(d2, dtype=jnp.float32) / d2)
    ang = pos[:, None] * inv[None, :]
    cos = jnp.cos(ang)[:, None, :]
    sin = jnp.sin(ang)[:, None, :]
    x32 = x.astype(jnp.float32)
    x1, x2 = x32[..., :d2], x32[..., d2:]
    return jnp.concatenate([x1 * cos - x2 * sin, x2 * cos + x1 * sin], axis=-1).astype(x.dtype)


def ret_log_decay():
    return jnp.log1p(-jnp.exp2(-5.0 - jnp.arange(RET_HEADS, dtype=jnp.float32)))


def ret_chunk(R, q, k, v, log_g):
    C = q.shape[1]
    n = jnp.arange(C, dtype=jnp.float32)
    diff = n[:, None] - n[None, :]
    causal = diff >= 0
    decay = jnp.where(causal[None], jnp.exp(jnp.where(causal, diff, 0.0)[None] * log_g[:, None, None]), 0.0)
    q32, k32, v32 = q.astype(jnp.float32), k.astype(jnp.float32), v.astype(jnp.float32)
    s = jnp.einsum('bnhd,bmhd->bhnm', q32, k32) * decay[None]
    o = jnp.einsum('bhnm,bmhe->bnhe', s, v32)
    o = o + jnp.einsum('bnhd,bhde->bnhe', q32, R) * jnp.exp((n + 1.0)[:, None] * log_g[None, :])[None, :, :, None]
    w_end = jnp.exp((C - 1.0 - n)[:, None] * log_g[None, :])
    R_new = R * jnp.exp(C * log_g)[None, :, None, None] + jnp.einsum('bmhd,bmhe->bhde', k32 * w_end[None, :, :, None], v32)
    return R_new, o


def retention(x, pos, state, w_in, gn_w, gn_b, w_out, chunk):
    B, L, _ = x.shape
    nc = L // chunk
    h = x @ w_in
    q = rope(h[..., :RET_QK_W].reshape(B, L, RET_HEADS, RET_DK), pos)
    k = rope(h[..., RET_QK_W:2 * RET_QK_W].reshape(B, L, RET_HEADS, RET_DK), pos) * (RET_DK ** -0.5)
    v = h[..., 2 * RET_QK_W:2 * RET_QK_W + RET_V_W].reshape(B, L, RET_HEADS, RET_DV)
    g = h[..., 2 * RET_QK_W + RET_V_W:]
    to_chunks = lambda t: t.reshape(B, nc, chunk, *t.shape[2:]).swapaxes(0, 1)
    log_g = ret_log_decay()
    R_fin, oc = lax.scan(lambda R, xs: ret_chunk(R, xs[0], xs[1], xs[2], log_g),
                         state.astype(jnp.float32), (to_chunks(q), to_chunks(k), to_chunks(v)))
    o = oc.swapaxes(0, 1).reshape(B, L, RET_HEADS, RET_DV)
    mu = jnp.mean(o, axis=-1, keepdims=True)
    var = jnp.mean(jnp.square(o - mu), axis=-1, keepdims=True)
    o = ((o - mu) * lax.rsqrt(var + GN_EPS)).reshape(B, L, RET_V_W) * gn_w.astype(jnp.float32) + gn_b.astype(jnp.float32)
    y = (o.astype(x.dtype) * jax.nn.silu(g)) @ w_out
    return y, R_fin


def mla_project(x, pos, w_in, q_norm, kv_norm, w_uq):
    B, L, _ = x.shape
    h = x @ w_in
    cq = rmsnorm(h[..., :Q_LORA], q_norm)
    ckv = rmsnorm(h[..., Q_LORA:Q_LORA + KV_LORA], kv_norm)
    kr = rope(h[..., Q_LORA + KV_LORA:][:, :, None, :], pos)[:, :, 0, :]
    q = (cq @ w_uq).reshape(B, L, MLA_HEADS, NOPE_DIM + ROPE_DIM)
    qn = q[..., :NOPE_DIM]
    qr = rope(q[..., NOPE_DIM:], pos)
    return qn, qr, ckv, kr


def mla_prompt(qn, qr, ckv, kr, w_uk, w_uv):
    B, S = qn.shape[:2]
    nb = S // Q_BLOCK
    scale = (NOPE_DIM + ROPE_DIM) ** -0.5
    key_pos = jnp.arange(S)
    ckv32 = ckv.astype(jnp.float32)
    w_uv32 = w_uv.astype(jnp.float32)

    def block(args):
        qn_b, qr_b, start = args
        q_lat = jnp.einsum('bqhn,chn->bqhc', qn_b, w_uk)
        s = (jnp.einsum('bqhc,bkc->bhqk', q_lat, ckv) + jnp.einsum('bqhr,bkr->bhqk', qr_b, kr)).astype(jnp.float32) * scale
        q_pos = start + jnp.arange(Q_BLOCK)
        s = jnp.where((key_pos[None, :] <= q_pos[:, None])[None, None], s, NEG)
        p = jax.nn.softmax(s, axis=-1)
        o_lat = jnp.einsum('bhqk,bkc->bqhc', p, ckv32)
        return jnp.einsum('bqhc,chv->bqhv', o_lat, w_uv32).astype(qn.dtype)

    blocks = lambda t: t.reshape(B, nb, Q_BLOCK, *t.shape[2:]).swapaxes(0, 1)
    o = lax.map(block, (blocks(qn), blocks(qr), jnp.arange(nb) * Q_BLOCK))
    return o.swapaxes(0, 1).reshape(B, S, MLA_HEADS * V_DIM)


def online_update(carry, s, vals):
    m, l, acc = carry
    m_new = jnp.maximum(m, jnp.max(s, axis=-1))
    corr = jnp.exp(m - m_new)
    p = jnp.exp(s - m_new[..., None])
    l = l * corr + jnp.sum(p, axis=-1)
    acc = acc * corr[..., None] + jnp.einsum('blhk,bkc->blhc', p, vals)
    return (m_new, l, acc)


def mla_sample(qn, qr, ckv_new, kr_new, cache_ckv, cache_kr, layer, page_table, w_uk, w_uv):
    B, L = qn.shape[:2]
    scale = (NOPE_DIM + ROPE_DIM) ** -0.5
    q_lat = jnp.einsum('blhn,chn->blhc', qn, w_uk).astype(jnp.float32)
    qr32 = qr.astype(jnp.float32)

    def page_step(carry, ids):
        ckv_p = cache_ckv[layer, ids].astype(jnp.float32)
        kr_p = cache_kr[layer, ids].astype(jnp.float32)
        s = (jnp.einsum('blhc,bpc->blhp', q_lat, ckv_p) + jnp.einsum('blhr,bpr->blhp', qr32, kr_p)) * scale
        return online_update(carry, s, ckv_p), None

    init = (jnp.full((B, L, MLA_HEADS), NEG, jnp.float32),
            jnp.zeros((B, L, MLA_HEADS), jnp.float32),
            jnp.zeros((B, L, MLA_HEADS, KV_LORA), jnp.float32))
    carry, _ = lax.scan(page_step, init, page_table.T)
    ckv32 = ckv_new.astype(jnp.float32)
    s = (jnp.einsum('blhc,bmc->blhm', q_lat, ckv32) + jnp.einsum('blhr,bmr->blhm', qr32, kr_new.astype(jnp.float32))) * scale
    causal = jnp.arange(L)[None, :] <= jnp.arange(L)[:, None]
    s = jnp.where(causal[None, :, None, :], s, NEG)
    _, l, acc = online_update(carry, s, ckv32)
    o_lat = acc / l[..., None]
    o = jnp.einsum('blhc,chv->blhv', o_lat, w_uv.astype(jnp.float32)).astype(qn.dtype)
    return o.reshape(B, L, MLA_HEADS * V_DIM)


def sq_relu_mlp(x, w_up, w_down):
    return jnp.square(jax.nn.relu(x @ w_up)) @ w_down


def setup_inputs(seed: int = 0) -> dict:
    key = jax.random.key(seed)
    ks = jax.random.split(key, 24)
    n_pages = PAST_LEN // PAGE_SIZE
    n_used = DEC_BATCH * n_pages
    n_pool = n_used + n_used // 4
    f32 = jnp.float32

    def dense(k, shape, fan_in):
        return jax.random.normal(k, shape, f32) * (fan_in ** -0.5)

    def gain(k, shape):
        return 1.0 + 0.01 * jax.random.normal(k, shape, f32)

    page_table = jax.random.permutation(ks[5], n_pool)[:n_used].reshape(DEC_BATCH, n_pages).astype(jnp.int32)
    return {
        "x_prompt": jax.random.normal(ks[0], (BATCH, SEQ, D_MODEL), f32),
        "x_sample": jax.random.normal(ks[1], (DEC_BATCH, DEC_SEQ, D_MODEL), f32),
        "state_ret": 0.1 * jax.random.normal(ks[2], (N_RET_LAYERS, DEC_BATCH, RET_HEADS, RET_DK, RET_DV), f32),
        "cache_kv_latent": jax.random.normal(ks[3], (N_MLA_LAYERS, n_pool, PAGE_SIZE, KV_LORA), f32),
        "cache_k_rope": jax.random.normal(ks[4], (N_MLA_LAYERS, n_pool, PAGE_SIZE, ROPE_DIM), f32),
        "page_table": page_table,
        "norm_mix": gain(ks[6], (DEPTH, D_MODEL)),
        "norm_mlp": gain(ks[7], (DEPTH, D_MODEL)),
        "norm_final": gain(ks[8], (D_MODEL,)),
        "w_ret_in": dense(ks[9], (N_RET_LAYERS, D_MODEL, 2 * RET_QK_W + 2 * RET_V_W), D_MODEL),
        "ret_gn_w": gain(ks[10], (N_RET_LAYERS, RET_V_W)),
        "ret_gn_b": 0.01 * jax.random.normal(ks[11], (N_RET_LAYERS, RET_V_W), f32),
        "w_ret_out": dense(ks[12], (N_RET_LAYERS, RET_V_W, D_MODEL), RET_V_W),
        "w_mla_in": dense(ks[13], (N_MLA_LAYERS, D_MODEL, MLA_IN_W), D_MODEL),
        "mla_q_norm": gain(ks[14], (N_MLA_LAYERS, Q_LORA)),
        "mla_kv_norm": gain(ks[15], (N_MLA_LAYERS, KV_LORA)),
        "w_mla_uq": dense(ks[16], (N_MLA_LAYERS, Q_LORA, MLA_HEADS * (NOPE_DIM + ROPE_DIM)), Q_LORA),
        "w_mla_uk": dense(ks[17], (N_MLA_LAYERS, KV_LORA, MLA_HEADS, NOPE_DIM), KV_LORA),
        "w_mla_uv": dense(ks[18], (N_MLA_LAYERS, KV_LORA, MLA_HEADS, V_DIM), KV_LORA),
        "w_mla_out": dense(ks[19], (N_MLA_LAYERS, MLA_HEADS * V_DIM, D_MODEL), MLA_HEADS * V_DIM),
        "w_up": dense(ks[20], (DEPTH, D_MODEL, D_FF), D_MODEL),
        "w_down": dense(ks[21], (DEPTH, D_FF, D_MODEL), D_FF),
    }


def reference(x_prompt, x_sample, state_ret, cache_kv_latent, cache_k_rope, page_table,
              norm_mix, norm_mlp, norm_final, w_ret_in, ret_gn_w, ret_gn_b, w_ret_out,
              w_mla_in, mla_q_norm, mla_kv_norm, w_mla_uq, w_mla_uk, w_mla_uv, w_mla_out,
              w_up, w_down):
    Bp, S = x_prompt.shape[0], x_prompt.shape[1]
    Ls = x_sample.shape[1]
    past = page_table.shape[1] * PAGE_SIZE
    pos_p = jnp.arange(S, dtype=jnp.float32)
    pos_s = past + jnp.arange(Ls, dtype=jnp.float32)
    xp, xs = x_prompt, x_sample
    ret_p, ret_s, ckv_p, kr_p, ckv_s, kr_s = [], [], [], [], [], []
    for i in range(DEPTH):
        j = i // 2
        hp = rmsnorm(xp, norm_mix[i])
        hs = rmsnorm(xs, norm_mix[i])
        if i % 2 == 0:
            zero_state = jnp.zeros((Bp, RET_HEADS, RET_DK, RET_DV), jnp.float32)
            yp, Rp = retention(hp, pos_p, zero_state, w_ret_in[j], ret_gn_w[j], ret_gn_b[j], w_ret_out[j], RET_CHUNK)
            ys, Rs = retention(hs, pos_s, state_ret[j], w_ret_in[j], ret_gn_w[j], ret_gn_b[j], w_ret_out[j], Ls)
            ret_p.append(Rp)
            ret_s.append(Rs)
        else:
            qn, qr, ckv, kr = mla_project(hp, pos_p, w_mla_in[j], mla_q_norm[j], mla_kv_norm[j], w_mla_uq[j])
            yp = mla_prompt(qn, qr, ckv, kr, w_mla_uk[j], w_mla_uv[j]) @ w_mla_out[j]
            ckv_p.append(ckv.reshape(Bp, S // PAGE_SIZE, PAGE_SIZE, KV_LORA))
            kr_p.append(kr.reshape(Bp, S // PAGE_SIZE, PAGE_SIZE, ROPE_DIM))
            qn, qr, ckv, kr = mla_project(hs, pos_s, w_mla_in[j], mla_q_norm[j], mla_kv_norm[j], w_mla_uq[j])
            ys = mla_sample(qn, qr, ckv, kr, cache_kv_latent, cache_k_rope, j, page_table, w_mla_uk[j], w_mla_uv[j]) @ w_mla_out[j]
            ckv_s.append(ckv)
            kr_s.append(kr)
        xp = xp + yp
        xs = xs + ys
        xp = xp + sq_relu_mlp(rmsnorm(xp, norm_mlp[i]), w_up[i], w_down[i])
        xs = xs + sq_relu_mlp(rmsnorm(xs, norm_mlp[i]), w_up[i], w_down[i])
    y_prompt = rmsnorm(xp, norm_final)
    y_sample = rmsnorm(xs, norm_final)
    return (y_prompt, y_sample, jnp.stack(ret_p), jnp.stack(ret_s), jnp.stack(ckv_p), jnp.stack(kr_p), jnp.stack(ckv_s), jnp.stack(kr_s))
```

```python
import functools

import jax
import jax.numpy as jnp
from jax import lax
from jax.experimental import pallas as pl
from jax.experimental.pallas import tpu as pltpu

ROPE_BASE = 10000.0
NORM_EPS = 1e-6
GN_EPS = 1e-5
NEG = -1e30
RET_CHUNK = 128

LANES = 128
V7X_VMEM_BYTES = 64 * 1024 * 1024
VMEM_LIMIT = 52 * 1024 * 1024

F32 = jnp.float32
BF16 = jnp.bfloat16


def _params(*sem):
    return pltpu.CompilerParams(dimension_semantics=sem, vmem_limit_bytes=VMEM_LIMIT)


def _tile(n, pref):
    if n <= pref:
        return n
    t = pref
    while n % t:
        t //= 2
    return t


def _nt_dot(a, b):
    return lax.dot_general(a, b, (((1,), (1,)), ((), ())), preferred_element_type=F32)


def _tn_dot(a, b):
    return lax.dot_general(a, b, (((0,), (0,)), ((), ())), preferred_element_type=F32)


def _rope_cos_sin(pos, d2):
    inv = ROPE_BASE ** (-jnp.arange(d2, dtype=F32) / d2)
    ang = pos[:, None] * inv[None, :]
    return jnp.cos(ang), jnp.sin(ang)


def _rope64_tables(pos):
    cos, sin = _rope_cos_sin(pos, 32)
    z = jnp.zeros_like(sin)
    cos_t = jnp.concatenate([cos, cos, cos, cos], axis=-1)
    sa = jnp.concatenate([-sin, z, -sin, z], axis=-1)
    sb = jnp.concatenate([z, sin, z, sin], axis=-1)
    return cos_t, sa, sb


def _rope64(x, cos_t, sa, sb):
    return x * cos_t + pltpu.roll(x, 96, 1) * sa + pltpu.roll(x, 32, 1) * sb


def _mm_kernel(*refs, has_norm, n_extra, epilogue):
    if has_norm:
        x_ref, g_ref, w_ref = refs[:3]
        rest = refs[3:]
        xn_ref = rest[-1]
        rest = rest[:-1]

        @pl.when(pl.program_id(1) == 0)
        def _():
            x = x_ref[...]
            ms = jnp.mean(x * x, axis=-1, keepdims=True)
            xn_ref[...] = (x * lax.rsqrt(ms + NORM_EPS) * g_ref[...]).astype(xn_ref.dtype)

        lhs = xn_ref[...]
    else:
        x_ref, w_ref = refs[:2]
        rest = refs[2:]
        lhs = x_ref[...]
    extra = rest[:n_extra]
    outs = rest[n_extra:]
    acc = jnp.dot(lhs, w_ref[...], preferred_element_type=F32)
    epilogue(pl.program_id(1), acc, extra, outs)


def _matmul(x, w, *, gain=None, extras=(), extra_specs=(), epilogue, out_shapes, out_specs, tm, tn):
    m, k = x.shape
    n = w.shape[1]
    has_norm = gain is not None
    in_specs = [pl.BlockSpec((tm, k), lambda i, j: (i, 0))]
    args = [x]
    if has_norm:
        in_specs.append(pl.BlockSpec((1, k), lambda i, j: (0, 0)))
        args.append(gain.reshape(1, k).astype(F32))
    in_specs.append(pl.BlockSpec((k, tn), lambda i, j: (0, j)))
    args.append(w)
    in_specs += list(extra_specs)
    args += list(extras)
    scratch = [pltpu.VMEM((tm, k), BF16)] if has_norm else []
    return pl.pallas_call(
        functools.partial(_mm_kernel, has_norm=has_norm, n_extra=len(extras), epilogue=epilogue),
        grid=(m // tm, n // tn),
        in_specs=in_specs,
        out_specs=out_specs,
        out_shape=out_shapes,
        scratch_shapes=scratch,
        compiler_params=_params("parallel", "arbitrary"),
    )(*args)


def _ret_in_epilogue(j, acc, extra, outs, *, tn, qk_w, v_w):
    cos_ref, sin_ref = extra
    (o_ref,) = outs
    n_qk = (2 * qk_w) // tn
    n_v = v_w // tn

    @pl.when(j < n_qk)
    def _():
        cos = cos_ref[...]
        sin = sin_ref[...]
        scale = jnp.where(j >= qk_w // tn, 0.0625, 1.0).astype(F32)
        for h in range(tn // 256):
            x1 = acc[:, h * 256:h * 256 + 128]
            x2 = acc[:, h * 256 + 128:(h + 1) * 256]
            o_ref[:, h * 256:h * 256 + 128] = ((x1 * cos - x2 * sin) * scale).astype(o_ref.dtype)
            o_ref[:, h * 256 + 128:(h + 1) * 256] = ((x2 * cos + x1 * sin) * scale).astype(o_ref.dtype)

    @pl.when(jnp.logical_and(j >= n_qk, j < n_qk + n_v))
    def _():
        o_ref[...] = acc.astype(o_ref.dtype)

    @pl.when(j >= n_qk + n_v)
    def _():
        o_ref[...] = (acc * jax.nn.sigmoid(acc)).astype(o_ref.dtype)


def _ret_in_proj(x, gain, w, cos, sin, *, qk_w, v_w, seq, out_dtype):
    m = x.shape[0]
    n = w.shape[1]
    tm = _tile(min(m, seq), 1024)
    tn = _tile(qk_w, 1024)
    nseq = seq // tm
    tab = pl.BlockSpec((tm, LANES), lambda i, j: (i % nseq, 0))
    (out,) = _matmul(
        x, w, gain=gain, extras=(cos, sin), extra_specs=(tab, tab),
        epilogue=functools.partial(_ret_in_epilogue, tn=tn, qk_w=qk_w, v_w=v_w),
        out_shapes=[jax.ShapeDtypeStruct((m, n), out_dtype)],
        out_specs=[pl.BlockSpec((tm, tn), lambda i, j: (i, j))],
        tm=tm, tn=tn)
    return out


def _ret_log_decay(heads):
    return jnp.log1p(-jnp.exp2(-5.0 - jnp.arange(heads, dtype=F32)))


def _group_norm_gate(o, gw, gb, gate):
    mu = jnp.mean(o, axis=-1, keepdims=True)
    d = o - mu
    var = jnp.mean(d * d, axis=-1, keepdims=True)
    return (d * lax.rsqrt(var + GN_EPS) * gw + gb) * gate


def _ret_chunk_kernel(q_ref, k_ref, v_ref, g_ref, dec_ref, rs_ref, we_ref, gc_ref, gw_ref, gb_ref,
                      y_ref, st_ref, r_ref):
    c = pl.program_id(2)

    @pl.when(c == 0)
    def _():
        r_ref[...] = jnp.zeros_like(r_ref)

    q = q_ref[...]
    k = k_ref[...]
    v = v_ref[...]
    dv = v.shape[1]
    r = r_ref[...]
    s = (_nt_dot(q, k) * dec_ref[0]).astype(BF16)
    rs = jnp.concatenate([rs_ref[0]] * (dv // LANES), axis=1)
    o = jnp.dot(s, v, preferred_element_type=F32)
    o = o + jnp.dot(q, r.astype(BF16), preferred_element_type=F32) * rs
    we = jnp.concatenate([we_ref[0]] * (k.shape[1] // LANES), axis=1)
    kw = (k.astype(F32) * we).astype(BF16)
    gc = jnp.concatenate([gc_ref[0, 0:1, :]] * (dv // LANES), axis=1)
    r_new = r * gc + _tn_dot(kw, v)
    r_ref[...] = r_new
    y_ref[...] = _group_norm_gate(o, gw_ref[...], gb_ref[...], g_ref[...].astype(F32)).astype(y_ref.dtype)

    @pl.when(c == pl.num_programs(2) - 1)
    def _():
        st_ref[0, 0] = r_new


def _ret_prompt(hq, gn_w, gn_b, *, batch, seq, heads, dk, dv):
    chunk = RET_CHUNK
    nc = seq // chunk
    qk_w = heads * dk
    v_w = heads * dv
    log_g = _ret_log_decay(heads)
    n = jnp.arange(chunk, dtype=F32)
    diff = n[:, None] - n[None, :]
    causal = diff >= 0
    decay = jnp.where(causal[None], jnp.exp(jnp.where(causal, diff, 0.0)[None] * log_g[:, None, None]), 0.0)
    ones = jnp.ones((1, 1, LANES), F32)
    rowscale = jnp.exp((n + 1.0)[None, :] * log_g[:, None])[:, :, None] * ones
    w_end = jnp.exp((chunk - 1.0 - n)[None, :] * log_g[:, None])[:, :, None] * ones
    g_chunk = jnp.exp(chunk * log_g)[:, None, None] * jnp.ones((1, 8, LANES), F32)

    row = lambda b, h, c: b * nc + c
    tab = lambda shape: pl.BlockSpec(shape, lambda b, h, c: (h, 0, 0))
    y, state = pl.pallas_call(
        _ret_chunk_kernel,
        grid=(batch, heads, nc),
        in_specs=[
            pl.BlockSpec((chunk, dk), lambda b, h, c: (row(b, h, c), h)),
            pl.BlockSpec((chunk, dk), lambda b, h, c: (row(b, h, c), heads + h)),
            pl.BlockSpec((chunk, dv), lambda b, h, c: (row(b, h, c), (2 * qk_w) // dv + h)),
            pl.BlockSpec((chunk, dv), lambda b, h, c: (row(b, h, c), (2 * qk_w + v_w) // dv + h)),
            tab((1, chunk, chunk)), tab((1, chunk, LANES)), tab((1, chunk, LANES)), tab((1, 8, LANES)),
            pl.BlockSpec((1, dv), lambda b, h, c: (0, h)),
            pl.BlockSpec((1, dv), lambda b, h, c: (0, h)),
        ],
        out_specs=[
            pl.BlockSpec((chunk, dv), lambda b, h, c: (row(b, h, c), h)),
            pl.BlockSpec((1, 1, dk, dv), lambda b, h, c: (b, h, 0, 0)),
        ],
        out_shape=[
            jax.ShapeDtypeStruct((batch * seq, v_w), BF16),
            jax.ShapeDtypeStruct((batch, heads, dk, dv), F32),
        ],
        scratch_shapes=[pltpu.VMEM((dk, dv), F32)],
        compiler_params=_params("parallel", "parallel", "arbitrary"),
    )(hq, hq, hq, hq, decay, rowscale, w_end, g_chunk,
      gn_w.reshape(1, v_w).astype(F32), gn_b.reshape(1, v_w).astype(F32))
    return y, state


def _column(row):
    n = row.shape[1]
    eye = lax.broadcasted_iota(jnp.int32, (n, n), 0) == lax.broadcasted_iota(jnp.int32, (n, n), 1)
    return jnp.sum(jnp.where(eye, row, 0.0), axis=1, keepdims=True)


def _ret_sample_kernel(gam_ref, h_ref, st_ref, gw_ref, gb_ref, y_ref, ns_ref, *, heads, dk, dv):
    qk_w = heads * dk
    v_w = heads * dv
    for h in range(heads):
        gam = gam_ref[h]
        q = h_ref[0, :, h * dk:(h + 1) * dk]
        k = h_ref[0, :, qk_w + h * dk:qk_w + (h + 1) * dk]
        v = h_ref[0, :, 2 * qk_w + h * dv:2 * qk_w + (h + 1) * dv]
        g = h_ref[0, :, 2 * qk_w + v_w + h * dv:2 * qk_w + v_w + (h + 1) * dv]
        r = st_ref[0, 0, h]
        s = jnp.sum(q * k, axis=-1, keepdims=True)
        o = s * v + jnp.sum(r * _column(q), axis=0, keepdims=True) * gam
        ns_ref[0, 0, h] = r * gam + _column(k) * v
        gw = gw_ref[:, h * dv:(h + 1) * dv]
        gb = gb_ref[:, h * dv:(h + 1) * dv]
        y_ref[0, :, h * dv:(h + 1) * dv] = _group_norm_gate(o, gw, gb, g).astype(y_ref.dtype)


def _ret_sample(hq, state, gn_w, gn_b, *, layer, heads, dk, dv):
    batch = hq.shape[0]
    n = hq.shape[1]
    v_w = heads * dv
    gam = jnp.exp(1.0 * _ret_log_decay(heads))
    y, new_state = pl.pallas_call(
        functools.partial(_ret_sample_kernel, heads=heads, dk=dk, dv=dv),
        grid=(batch,),
        in_specs=[
            pl.BlockSpec(memory_space=pltpu.SMEM),
            pl.BlockSpec((1, 1, n), lambda b: (b, 0, 0)),
            pl.BlockSpec((1, 1, heads, dk, dv), lambda b: (layer, b, 0, 0, 0)),
            pl.BlockSpec((1, v_w), lambda b: (0, 0)),
            pl.BlockSpec((1, v_w), lambda b: (0, 0)),
        ],
        out_specs=[
            pl.BlockSpec((1, 1, v_w), lambda b: (b, 0, 0)),
            pl.BlockSpec((1, 1, heads, dk, dv), lambda b: (0, b, 0, 0, 0)),
        ],
        out_shape=[
            jax.ShapeDtypeStruct((batch, 1, v_w), BF16),
            jax.ShapeDtypeStruct((1,) + state.shape[1:], F32),
        ],
        compiler_params=_params("parallel"),
    )(gam, hq.reshape(batch, 1, n), state,
      gn_w.reshape(1, v_w).astype(F32), gn_b.reshape(1, v_w).astype(F32))
    return y.reshape(batch, v_w), new_state


def _residual_epilogue(j, acc, extra, outs):
    (res_ref,) = extra
    (o_ref,) = outs
    o_ref[...] = res_ref[...] + acc


def _proj_residual(a, w, res):
    m = a.shape[0]
    n = w.shape[1]
    tm = _tile(m, 512)
    tn = _tile(n, 1024)
    blk = pl.BlockSpec((tm, tn), lambda i, j: (i, j))
    (out,) = _matmul(a, w, extras=(res,), extra_specs=(blk,), epilogue=_residual_epilogue,
                     out_shapes=[jax.ShapeDtypeStruct((m, n), F32)], out_specs=[blk], tm=tm, tn=tn)
    return out


def _mlp_kernel(x_ref, g_ref, wu_ref, wd_ref, o_ref, xn_ref):
    f = pl.program_id(1)

    @pl.when(f == 0)
    def _():
        x = x_ref[...]
        ms = jnp.mean(x * x, axis=-1, keepdims=True)
        xn_ref[...] = (x * lax.rsqrt(ms + NORM_EPS) * g_ref[...]).astype(xn_ref.dtype)
        o_ref[...] = x

    h = jnp.dot(xn_ref[...], wu_ref[...], preferred_element_type=F32)
    h = jnp.square(jnp.maximum(h, 0.0)).astype(BF16)
    o_ref[...] += jnp.dot(h, wd_ref[...], preferred_element_type=F32)


def _mlp_residual(x, gain, w_up, w_down):
    m, d = x.shape
    ff = w_up.shape[1]
    tm = _tile(m, 512)
    tf = _tile(ff, 512)
    return pl.pallas_call(
        _mlp_kernel,
        grid=(m // tm, ff // tf),
        in_specs=[
            pl.BlockSpec((tm, d), lambda i, f: (i, 0)),
            pl.BlockSpec((1, d), lambda i, f: (0, 0)),
            pl.BlockSpec((d, tf), lambda i, f: (0, f)),
            pl.BlockSpec((tf, d), lambda i, f: (f, 0)),
        ],
        out_specs=pl.BlockSpec((tm, d), lambda i, f: (i, 0)),
        out_shape=jax.ShapeDtypeStruct((m, d), F32),
        scratch_shapes=[pltpu.VMEM((tm, d), BF16)],
        compiler_params=_params("parallel", "arbitrary"),
    )(x, gain.reshape(1, d).astype(F32), w_up, w_down)


def _rmsnorm_kernel(x_ref, g_ref, o_ref):
    x = x_ref[...]
    ms = jnp.mean(x * x, axis=-1, keepdims=True)
    o_ref[...] = x * lax.rsqrt(ms + NORM_EPS) * g_ref[...]


def _rmsnorm(x, gain):
    m, d = x.shape
    tm = _tile(m, 1024)
    return pl.pallas_call(
        _rmsnorm_kernel,
        grid=(m // tm,),
        in_specs=[pl.BlockSpec((tm, d), lambda i: (i, 0)), pl.BlockSpec((1, d), lambda i: (0, 0))],
        out_specs=pl.BlockSpec((tm, d), lambda i: (i, 0)),
        out_shape=jax.ShapeDtypeStruct((m, d), F32),
        compiler_params=_params("parallel"),
    )(x, gain.reshape(1, d).astype(F32))


def _mla_in_epilogue(j, acc, extra, outs, *, q_lora, kv_lora):
    qg_ref, kg_ref, cos_ref, sa_ref, sb_ref = extra
    cq_ref, ckv_ref, ckvb_ref, kr_ref, krb_ref = outs

    def norm(x, g):
        return x * lax.rsqrt(jnp.mean(x * x, axis=-1, keepdims=True) + NORM_EPS) * g

    cq_ref[...] = norm(acc[:, :q_lora], qg_ref[...]).astype(cq_ref.dtype)
    ckv = norm(acc[:, q_lora:q_lora + kv_lora], kg_ref[...])
    ckv_ref[...] = ckv
    ckvb_ref[...] = ckv.astype(ckvb_ref.dtype)
    kr = _rope64(acc[:, q_lora + kv_lora:], cos_ref[...], sa_ref[...], sb_ref[...])
    kr_ref[...] = kr
    krb_ref[...] = kr.astype(krb_ref.dtype)


def _mla_in_proj(x, gain, w, q_gain, kv_gain, tabs, *, q_lora, kv_lora, seq):
    m = x.shape[0]
    n = w.shape[1]
    tm = _tile(min(m, seq), 512)
    nseq = seq // tm
    tab = pl.BlockSpec((tm, LANES), lambda i, j: (i % nseq, 0))
    vec = lambda width: pl.BlockSpec((1, width), lambda i, j: (0, 0))
    blk = lambda width: pl.BlockSpec((tm, width), lambda i, j: (i, 0))
    return _matmul(
        x, w, gain=gain,
        extras=(q_gain.reshape(1, q_lora).astype(F32), kv_gain.reshape(1, kv_lora).astype(F32)) + tuple(tabs),
        extra_specs=(vec(q_lora), vec(kv_lora), tab, tab, tab),
        epilogue=functools.partial(_mla_in_epilogue, q_lora=q_lora, kv_lora=kv_lora),
        out_shapes=[jax.ShapeDtypeStruct((m, q_lora), BF16),
                    jax.ShapeDtypeStruct((m, kv_lora), F32),
                    jax.ShapeDtypeStruct((m, kv_lora), BF16),
                    jax.ShapeDtypeStruct((m, LANES), F32),
                    jax.ShapeDtypeStruct((m, LANES), BF16)],
        out_specs=[blk(q_lora), blk(kv_lora), blk(kv_lora), blk(LANES), blk(LANES)],
        tm=tm, tn=n)


def _mla_q_epilogue(j, acc, extra, outs, *, n_nope, scale):
    cos_ref, sa_ref, sb_ref = extra
    (o_ref,) = outs

    @pl.when(j < n_nope)
    def _():
        o_ref[...] = (acc * scale).astype(o_ref.dtype)

    @pl.when(j >= n_nope)
    def _():
        cos = cos_ref[...]
        sa = sa_ref[...]
        sb = sb_ref[...]
        for h in range(acc.shape[1] // LANES):
            x = acc[:, h * LANES:(h + 1) * LANES]
            o_ref[:, h * LANES:(h + 1) * LANES] = (_rope64(x, cos, sa, sb) * scale).astype(o_ref.dtype)


def _mla_q_proj(cq, w, tabs, *, nope_w, scale, seq, out_dtype):
    m = cq.shape[0]
    n = w.shape[1]
    tm = _tile(min(m, seq), 1024)
    tn = _tile(nope_w, 1024)
    nseq = seq // tm
    tab = pl.BlockSpec((tm, LANES), lambda i, j: (i % nseq, 0))
    (out,) = _matmul(
        cq, w, extras=tuple(tabs), extra_specs=(tab, tab, tab),
        epilogue=functools.partial(_mla_q_epilogue, n_nope=nope_w // tn, scale=scale),
        out_shapes=[jax.ShapeDtypeStruct((m, n), out_dtype)],
        out_specs=[pl.BlockSpec((tm, tn), lambda i, j: (i, j))],
        tm=tm, tn=tn)
    return out


def _cast_epilogue(j, acc, extra, outs):
    (o_ref,) = outs
    o_ref[...] = acc.astype(o_ref.dtype)


def _plain_matmul(a, w, out_dtype):
    m = a.shape[0]
    n = w.shape[1]
    tm = _tile(m, 1024)
    tn = _tile(n, 1024)
    (out,) = _matmul(a, w, epilogue=_cast_epilogue,
                     out_shapes=[jax.ShapeDtypeStruct((m, n), out_dtype)],
                     out_specs=[pl.BlockSpec((tm, tn), lambda i, j: (i, j))], tm=tm, tn=tn)
    return out


def _flash_kernel(qn_ref, qr_ref, kn_ref, kr_ref, v_ref, o_ref, *, tq):
    qi = pl.program_id(2)
    q = jnp.concatenate([qn_ref[...], qr_ref[...]], axis=1)

    def scores(start):
        k = jnp.concatenate([kn_ref[pl.ds(start, tq), :], kr_ref[pl.ds(start, tq), :]], axis=1)
        return _nt_dot(q, k)

    def update(carry, s, start):
        m, l, acc = carry
        m_new = jnp.maximum(m, jnp.max(s, axis=-1, keepdims=True))
        corr = jnp.exp(m - m_new)
        p = jnp.exp(s - m_new)
        l = l * corr + jnp.sum(p, axis=-1, keepdims=True)
        acc = acc * corr + jnp.dot(p.astype(BF16), v_ref[pl.ds(start, tq), :], preferred_element_type=F32)
        return m_new, l, acc

    def body(kb, carry):
        start = pl.multiple_of(kb * tq, tq)
        return update(carry, scores(start), start)

    init = (jnp.full((tq, 1), NEG, F32), jnp.zeros((tq, 1), F32), jnp.zeros((tq, v_ref.shape[1]), F32))
    carry = lax.fori_loop(0, qi, body, init)
    start = pl.multiple_of(qi * tq, tq)
    causal = lax.broadcasted_iota(jnp.int32, (tq, tq), 1) <= lax.broadcasted_iota(jnp.int32, (tq, tq), 0)
    _, l, acc = update(carry, jnp.where(causal, scores(start), NEG), start)
    o_ref[...] = (acc / l).astype(o_ref.dtype)


def _mla_prompt_attention(q, kv, krb, *, batch, seq, heads):
    m = q.shape[0]
    tq = _tile(seq, 512)
    nq = seq // tq
    qblk = lambda off: pl.BlockSpec((tq, LANES), lambda b, h, i: (b * nq + i, off + h))
    kblk = lambda off: pl.BlockSpec((seq, LANES), lambda b, h, i: (b, off + h))
    return pl.pallas_call(
        functools.partial(_flash_kernel, tq=tq),
        grid=(batch, heads, nq),
        in_specs=[qblk(0), qblk(heads), kblk(0), pl.BlockSpec((seq, LANES), lambda b, h, i: (b, 0)), kblk(heads)],
        out_specs=pl.BlockSpec((tq, LANES), lambda b, h, i: (b * nq + i, h)),
        out_shape=jax.ShapeDtypeStruct((m, heads * LANES), BF16),
        compiler_params=_params("parallel", "parallel", "arbitrary"),
    )(q, q, kv, krb, kv)


def _head_nt_kernel(a_ref, w_ref, o_ref):
    o_ref[0] = _nt_dot(a_ref[...], w_ref[...]).astype(o_ref.dtype)


def _absorb_uk(qn, w_uk2, *, heads):
    b = qn.shape[0]
    c = w_uk2.shape[0]
    return pl.pallas_call(
        _head_nt_kernel,
        grid=(heads,),
        in_specs=[pl.BlockSpec((b, LANES), lambda h: (0, h)), pl.BlockSpec((c, LANES), lambda h: (0, h))],
        out_specs=pl.BlockSpec((1, b, c), lambda h: (h, 0, 0)),
        out_shape=jax.ShapeDtypeStruct((heads, b, c), F32),
        compiler_params=_params("parallel"),
    )(qn, w_uk2)


def _head_nn_kernel(a_ref, w_ref, o_ref):
    o_ref[...] = jnp.dot(a_ref[0], w_ref[...], preferred_element_type=F32).astype(o_ref.dtype)


def _expand_uv(o_lat, w_uv2, *, heads):
    b = o_lat.shape[1]
    c = o_lat.shape[2]
    return pl.pallas_call(
        _head_nn_kernel,
        grid=(heads,),
        in_specs=[pl.BlockSpec((1, b, c), lambda h: (h, 0, 0)), pl.BlockSpec((c, LANES), lambda h: (0, h))],
        out_specs=pl.BlockSpec((b, LANES), lambda h: (0, h)),
        out_shape=jax.ShapeDtypeStruct((b, heads * LANES), BF16),
        compiler_params=_params("parallel"),
    )(o_lat, w_uv2)


def _paged_kernel(pt_ref, ql_ref, qr_ref, cn_ref, kn_ref, *rest, pps, rope_dim):
    ckv_refs = rest[:pps]
    kr_refs = rest[pps:2 * pps]
    o_ref, m_ref, l_ref, acc_ref = rest[2 * pps:]
    g = pl.program_id(1)

    @pl.when(g == 0)
    def _():
        m_ref[...] = jnp.full_like(m_ref, NEG)
        l_ref[...] = jnp.zeros_like(l_ref)
        acc_ref[...] = jnp.zeros_like(acc_ref)

    ql = ql_ref[0]
    qr = qr_ref[0][:, :rope_dim]
    qlb = ql.astype(BF16)
    qrb = qr.astype(BF16)
    pages = [r[0, 0].astype(BF16) for r in ckv_refs]
    s = jnp.concatenate(
        [_nt_dot(qlb, pages[i]) + _nt_dot(qrb, kr_refs[i][0, 0].astype(BF16)) for i in range(pps)], axis=1)
    m = m_ref[...]
    m_new = jnp.maximum(m, jnp.max(s, axis=-1, keepdims=True))
    corr = jnp.exp(m - m_new)
    p = jnp.exp(s - m_new)
    l_new = l_ref[...] * corr + jnp.sum(p, axis=-1, keepdims=True)
    pb = p.astype(BF16)
    page = pages[0].shape[0]
    pv = jnp.dot(pb[:, :page], pages[0], preferred_element_type=F32)
    for i in range(1, pps):
        pv = pv + jnp.dot(pb[:, i * page:(i + 1) * page], pages[i], preferred_element_type=F32)
    acc_new = acc_ref[...] * corr + pv
    m_ref[...] = m_new
    l_ref[...] = l_new
    acc_ref[...] = acc_new

    @pl.when(g == pl.num_programs(1) - 1)
    def _():
        cn = cn_ref[0]
        s_new = (jnp.sum(ql * cn, axis=-1, keepdims=True)
                 + jnp.sum(qr * kn_ref[0][:, :rope_dim], axis=-1, keepdims=True))
        m_fin = jnp.maximum(m_new, s_new)
        c_fin = jnp.exp(m_new - m_fin)
        p_new = jnp.exp(s_new - m_fin)
        l_fin = l_new * c_fin + p_new
        o_ref[0] = (acc_new * c_fin + p_new * cn) / l_fin


def _mla_paged_attention(q_lat, qr, ckv_new, kr_new, cache_ckv, cache_kr, page_table, *, layer):
    b, heads, c = q_lat.shape
    n_pages = page_table.shape[1]
    page = cache_ckv.shape[2]
    rope_dim = cache_kr.shape[3]
    pps = _tile(n_pages, 8)
    per_b = lambda width: pl.BlockSpec((1, heads, width), lambda i, g, pt: (i, 0, 0))
    new_b = lambda width: pl.BlockSpec((1, 1, width), lambda i, g, pt: (i, 0, 0))

    def page_spec(width, k):
        return pl.BlockSpec((1, 1, page, width), lambda i, g, pt: (layer, pt[i, g * pps + k], 0, 0))

    grid_spec = pltpu.PrefetchScalarGridSpec(
        num_scalar_prefetch=1,
        grid=(b, n_pages // pps),
        in_specs=[per_b(c), per_b(LANES), new_b(c), new_b(LANES)]
        + [page_spec(c, k) for k in range(pps)] + [page_spec(rope_dim, k) for k in range(pps)],
        out_specs=per_b(c),
        scratch_shapes=[pltpu.VMEM((heads, 1), F32), pltpu.VMEM((heads, 1), F32), pltpu.VMEM((heads, c), F32)],
    )
    return pl.pallas_call(
        functools.partial(_paged_kernel, pps=pps, rope_dim=rope_dim),
        grid_spec=grid_spec,
        out_shape=jax.ShapeDtypeStruct((b, heads, c), F32),
        compiler_params=_params("parallel", "arbitrary"),
    )(page_table, q_lat, qr, ckv_new.reshape(b, 1, c), kr_new.reshape(b, 1, LANES),
      *([cache_ckv] * pps), *([cache_kr] * pps))


def _concat_layers(parts):
    return parts[0] if len(parts) == 1 else jnp.concatenate(parts, axis=0)


def kernel(x_prompt, x_sample, state_ret, cache_kv_latent, cache_k_rope, page_table, norm_mix, norm_mlp, norm_final, w_ret_in, ret_gn_w, ret_gn_b, w_ret_out, w_mla_in, mla_q_norm, mla_kv_norm, w_mla_uq, w_mla_uk, w_mla_uv, w_mla_out, w_up, w_down):
    bp, seq, d = x_prompt.shape
    bs, ls, _ = x_sample.shape
    assert ls == 1, "the sample group is a single new token per sequence"
    depth = norm_mix.shape[0]
    ret_heads, ret_dk, ret_dv = state_ret.shape[2:]
    qk_w = ret_heads * ret_dk
    v_w = ret_heads * ret_dv
    kv_lora, mla_heads, nope = w_mla_uk.shape[1:]
    v_dim = w_mla_uv.shape[3]
    q_lora = mla_q_norm.shape[1]
    rope_dim = cache_k_rope.shape[3]
    page = cache_kv_latent.shape[2]
    assert ret_dk == 2 * LANES and nope == LANES and v_dim == LANES and rope_dim * 2 == LANES
    past = page_table.shape[1] * page
    scale = float((nope + rope_dim) ** -0.5)

    pos_p = jnp.arange(seq, dtype=F32)
    pos_s = jnp.broadcast_to(past + jnp.arange(ls, dtype=F32), (bs,))
    ret_tab_p = _rope_cos_sin(pos_p, ret_dk // 2)
    ret_tab_s = _rope_cos_sin(pos_s, ret_dk // 2)
    mla_tab_p = _rope64_tables(pos_p)
    mla_tab_s = _rope64_tables(pos_s)

    xp = x_prompt.reshape(bp * seq, d)
    xs = x_sample.reshape(bs, d)
    ret_p, ret_s, ckv_p, kr_p, ckv_s, kr_s = [], [], [], [], [], []
    for i in range(depth):
        j = i // 2
        if i % 2 == 0:
            w_in = w_ret_in[j].astype(BF16)
            w_out = w_ret_out[j].astype(BF16)
            hq = _ret_in_proj(xp, norm_mix[i], w_in, *ret_tab_p, qk_w=qk_w, v_w=v_w, seq=seq, out_dtype=BF16)
            y, r_p = _ret_prompt(hq, ret_gn_w[j], ret_gn_b[j], batch=bp, seq=seq,
                                 heads=ret_heads, dk=ret_dk, dv=ret_dv)
            xp = _proj_residual(y, w_out, xp)
            hq = _ret_in_proj(xs, norm_mix[i], w_in, *ret_tab_s, qk_w=qk_w, v_w=v_w, seq=bs, out_dtype=F32)
            y, r_s = _ret_sample(hq, state_ret, ret_gn_w[j], ret_gn_b[j], layer=j,
                                 heads=ret_heads, dk=ret_dk, dv=ret_dv)
            xs = _proj_residual(y, w_out, xs)
            ret_p.append(r_p[None])
            ret_s.append(r_s)
        else:
            w_in = jnp.pad(w_mla_in[j], ((0, 0), (0, LANES - rope_dim))).astype(BF16)
            w_uq = w_mla_uq[j].reshape(q_lora, mla_heads, nope + rope_dim)
            w_uq = jnp.concatenate(
                [w_uq[:, :, :nope].reshape(q_lora, mla_heads * nope),
                 jnp.pad(w_uq[:, :, nope:], ((0, 0), (0, 0), (0, LANES - rope_dim))).reshape(q_lora, mla_heads * LANES)],
                axis=1).astype(BF16)
            w_uk2 = w_mla_uk[j].reshape(kv_lora, mla_heads * nope).astype(BF16)
            w_uv2 = w_mla_uv[j].reshape(kv_lora, mla_heads * v_dim).astype(BF16)
            w_out = w_mla_out[j].astype(BF16)
            nope_w = mla_heads * nope

            cq, ckv, ckvb, kr, krb = _mla_in_proj(xp, norm_mix[i], w_in, mla_q_norm[j], mla_kv_norm[j], mla_tab_p,
                                                  q_lora=q_lora, kv_lora=kv_lora, seq=seq)
            q = _mla_q_proj(cq, w_uq, mla_tab_p, nope_w=nope_w, scale=scale, seq=seq, out_dtype=BF16)
            kv = _plain_matmul(ckvb, jnp.concatenate([w_uk2, w_uv2], axis=1), BF16)
            o = _mla_prompt_attention(q, kv, krb, batch=bp, seq=seq, heads=mla_heads)
            xp = _proj_residual(o, w_out, xp)
            ckv_p.append(ckv.reshape(1, bp, seq // page, page, kv_lora))
            kr_p.append(kr[:, :rope_dim].reshape(1, bp, seq // page, page, rope_dim))

            cq, ckv, ckvb, kr, krb = _mla_in_proj(xs, norm_mix[i], w_in, mla_q_norm[j], mla_kv_norm[j], mla_tab_s,
                                                  q_lora=q_lora, kv_lora=kv_lora, seq=bs)
            q = _mla_q_proj(cq, w_uq, mla_tab_s, nope_w=nope_w, scale=scale, seq=bs, out_dtype=F32)
            q_lat = _absorb_uk(q[:, :nope_w].astype(BF16), w_uk2, heads=mla_heads)
            o_lat = _mla_paged_attention(
                q_lat.transpose(1, 0, 2), q[:, nope_w:].reshape(bs, mla_heads, LANES), ckv, kr,
                cache_kv_latent, cache_k_rope, page_table, layer=j)
            o = _expand_uv(o_lat.transpose(1, 0, 2).astype(BF16), w_uv2, heads=mla_heads)
            xs = _proj_residual(o, w_out, xs)
            ckv_s.append(ckv.reshape(1, bs, ls, kv_lora))
            kr_s.append(kr[:, :rope_dim].reshape(1, bs, ls, rope_dim))
        w_u = w_up[i].astype(BF16)
        w_d = w_down[i].astype(BF16)
        xp = _mlp_residual(xp, norm_mlp[i], w_u, w_d)
        xs = _mlp_residual(xs, norm_mlp[i], w_u, w_d)
    y_prompt = _rmsnorm(xp, norm_final).reshape(bp, seq, d)
    y_sample = _rmsnorm(xs, norm_final).reshape(bs, ls, d)
    return (y_prompt, y_sample) + tuple(_concat_layers(t) for t in (ret_p, ret_s, ckv_p, kr_p, ckv_s, kr_s))
```

```python
import functools

import jax
import jax.numpy as jnp
from jax import lax
from jax.experimental import pallas as pl
from jax.experimental.pallas import tpu as pltpu

ROPE_BASE = 10000.0
NORM_EPS = 1e-6
GN_EPS = 1e-5
NEG = -1e30
RET_CHUNK = 128
LOG2_E = 1.4426950408889634

LANES = 128
V7X_VMEM_BYTES = 64 * 1024 * 1024
VMEM_LIMIT = 52 * 1024 * 1024

F32 = jnp.float32
BF16 = jnp.bfloat16


def _params(*sem):
    return pltpu.CompilerParams(dimension_semantics=sem, vmem_limit_bytes=VMEM_LIMIT)


def _tile(n, pref):
    if n <= pref:
        return n
    t = pref
    while n % t:
        t //= 2
    return t


def _nt_dot(a, b):
    return lax.dot_general(a, b, (((1,), (1,)), ((), ())), preferred_element_type=F32)


def _tn_dot(a, b):
    return lax.dot_general(a, b, (((0,), (0,)), ((), ())), preferred_element_type=F32)


def _rope_cos_sin(pos, d2):
    inv = ROPE_BASE ** (-jnp.arange(d2, dtype=F32) / d2)
    ang = pos[:, None] * inv[None, :]
    return jnp.cos(ang), jnp.sin(ang)


def _rope64_tables(pos):
    cos, sin = _rope_cos_sin(pos, 32)
    z = jnp.zeros_like(sin)
    cos_t = jnp.concatenate([cos, cos, cos, cos], axis=-1)
    sa = jnp.concatenate([-sin, z, -sin, z], axis=-1)
    sb = jnp.concatenate([z, sin, z, sin], axis=-1)
    return cos_t, sa, sb


def _rope64(x, cos_t, sa, sb):
    return x * cos_t + pltpu.roll(x, 96, 1) * sa + pltpu.roll(x, 32, 1) * sb


def _mm_kernel(*refs, has_norm, n_extra, epilogue):
    if has_norm:
        x_ref, g_ref, w_ref = refs[:3]
        rest = refs[3:]
        xn_ref = rest[-1]
        rest = rest[:-1]

        @pl.when(pl.program_id(1) == 0)
        def _():
            x = x_ref[...]
            ms = jnp.mean(x * x, axis=-1, keepdims=True)
            xn_ref[...] = (x * lax.rsqrt(ms + NORM_EPS) * g_ref[...]).astype(xn_ref.dtype)

        lhs = xn_ref[...]
    else:
        x_ref, w_ref = refs[:2]
        rest = refs[2:]
        lhs = x_ref[...]
    extra = rest[:n_extra]
    outs = rest[n_extra:]
    acc = jnp.dot(lhs, w_ref[...], preferred_element_type=F32)
    epilogue(pl.program_id(1), acc, extra, outs)


def _matmul(x, w, *, name, gain=None, extras=(), extra_specs=(), epilogue, out_shapes, out_specs, tm, tn):
    m, k = x.shape
    n = w.shape[1]
    has_norm = gain is not None
    in_specs = [pl.BlockSpec((tm, k), lambda i, j: (i, 0))]
    args = [x]
    if has_norm:
        in_specs.append(pl.BlockSpec((1, k), lambda i, j: (0, 0)))
        args.append(gain.reshape(1, k).astype(F32))
    in_specs.append(pl.BlockSpec((k, tn), lambda i, j: (0, j)))
    args.append(w)
    in_specs += list(extra_specs)
    args += list(extras)
    scratch = [pltpu.VMEM((tm, k), BF16)] if has_norm else []
    return pl.pallas_call(
        functools.partial(_mm_kernel, has_norm=has_norm, n_extra=len(extras), epilogue=epilogue),
        grid=(m // tm, n // tn),
        in_specs=in_specs,
        out_specs=out_specs,
        out_shape=out_shapes,
        scratch_shapes=scratch,
        compiler_params=_params("parallel", "arbitrary"),
        name=name,
    )(*args)


def _ret_in_epilogue(j, acc, extra, outs, *, tn, qk_w, v_w):
    cos_ref, sin_ref = extra
    (o_ref,) = outs
    n_qk = (2 * qk_w) // tn
    n_v = v_w // tn

    @pl.when(j < n_qk)
    def _():
        cos = cos_ref[...]
        sin = sin_ref[...]
        scale = jnp.where(j >= qk_w // tn, 0.0625, 1.0).astype(F32)
        for h in range(tn // 256):
            x1 = acc[:, h * 256:h * 256 + 128]
            x2 = acc[:, h * 256 + 128:(h + 1) * 256]
            o_ref[:, h * 256:h * 256 + 128] = ((x1 * cos - x2 * sin) * scale).astype(o_ref.dtype)
            o_ref[:, h * 256 + 128:(h + 1) * 256] = ((x2 * cos + x1 * sin) * scale).astype(o_ref.dtype)

    @pl.when(jnp.logical_and(j >= n_qk, j < n_qk + n_v))
    def _():
        o_ref[...] = acc.astype(o_ref.dtype)

    @pl.when(j >= n_qk + n_v)
    def _():
        o_ref[...] = (acc * jax.nn.sigmoid(acc)).astype(o_ref.dtype)


def _ret_in_proj(x, gain, w, cos, sin, *, qk_w, v_w, seq, out_dtype):
    m = x.shape[0]
    n = w.shape[1]
    tm = _tile(min(m, seq), 1024)
    tn = _tile(qk_w, 1024)
    nseq = seq // tm
    tab = pl.BlockSpec((tm, LANES), lambda i, j: (i % nseq, 0))
    (out,) = _matmul(
        x, w, name="ret_in_proj", gain=gain, extras=(cos, sin), extra_specs=(tab, tab),
        epilogue=functools.partial(_ret_in_epilogue, tn=tn, qk_w=qk_w, v_w=v_w),
        out_shapes=[jax.ShapeDtypeStruct((m, n), out_dtype)],
        out_specs=[pl.BlockSpec((tm, tn), lambda i, j: (i, j))],
        tm=tm, tn=tn)
    return out


def _ret_log_decay(heads):
    return jnp.log1p(-jnp.exp2(-5.0 - jnp.arange(heads, dtype=F32)))


def _group_norm_gate(o, gw, gb, gate):
    mu = jnp.mean(o, axis=-1, keepdims=True)
    d = o - mu
    var = jnp.mean(d * d, axis=-1, keepdims=True)
    return (d * lax.rsqrt(var + GN_EPS) * gw + gb) * gate


def _ret_chunk_kernel(h_ref, dec_ref, rs_ref, we_ref, gc_ref, gw_ref, gb_ref, y_ref, st_ref, r_ref,
                      *, heads, dk, dv):
    c = pl.program_id(1)
    qk_w = heads * dk
    v_w = heads * dv

    @pl.when(c == 0)
    def _():
        r_ref[...] = jnp.zeros_like(r_ref)

    for h in range(heads):
        q = h_ref[:, h * dk:(h + 1) * dk]
        k = h_ref[:, qk_w + h * dk:qk_w + (h + 1) * dk]
        v = h_ref[:, 2 * qk_w + h * dv:2 * qk_w + (h + 1) * dv]
        g = h_ref[:, 2 * qk_w + v_w + h * dv:2 * qk_w + v_w + (h + 1) * dv]
        r = r_ref[h]
        s = (_nt_dot(q, k) * dec_ref[h]).astype(BF16)
        rs = jnp.concatenate([rs_ref[h]] * (dv // LANES), axis=1)
        o = jnp.dot(s, v, preferred_element_type=F32)
        o = o + jnp.dot(q, r.astype(BF16), preferred_element_type=F32) * rs
        we = jnp.concatenate([we_ref[h]] * (dk // LANES), axis=1)
        kw = (k.astype(F32) * we).astype(BF16)
        gc = jnp.concatenate([gc_ref[h, 0:1, :]] * (dv // LANES), axis=1)
        r_ref[h] = r * gc + _tn_dot(kw, v)
        gw = gw_ref[:, h * dv:(h + 1) * dv]
        gb = gb_ref[:, h * dv:(h + 1) * dv]
        y_ref[:, h * dv:(h + 1) * dv] = _group_norm_gate(o, gw, gb, g.astype(F32)).astype(y_ref.dtype)

    @pl.when(c == pl.num_programs(1) - 1)
    def _():
        st_ref[0] = r_ref[...]


def _ret_prompt(hq, gn_w, gn_b, *, batch, seq, heads, dk, dv):
    chunk = RET_CHUNK
    nc = seq // chunk
    qk_w = heads * dk
    v_w = heads * dv
    log_g = _ret_log_decay(heads)
    n = jnp.arange(chunk, dtype=F32)
    diff = n[:, None] - n[None, :]
    causal = diff >= 0
    decay = jnp.where(causal[None], jnp.exp(jnp.where(causal, diff, 0.0)[None] * log_g[:, None, None]), 0.0)
    ones = jnp.ones((1, 1, LANES), F32)
    rowscale = jnp.exp((n + 1.0)[None, :] * log_g[:, None])[:, :, None] * ones
    w_end = jnp.exp((chunk - 1.0 - n)[None, :] * log_g[:, None])[:, :, None] * ones
    g_chunk = jnp.exp(chunk * log_g)[:, None, None] * jnp.ones((1, 8, LANES), F32)

    tab = lambda shape: pl.BlockSpec(shape, lambda b, c: (0, 0, 0))
    vec = pl.BlockSpec((1, v_w), lambda b, c: (0, 0))
    y, state = pl.pallas_call(
        functools.partial(_ret_chunk_kernel, heads=heads, dk=dk, dv=dv),
        grid=(batch, nc),
        in_specs=[
            pl.BlockSpec((chunk, hq.shape[1]), lambda b, c: (b * nc + c, 0)),
            tab((heads, chunk, chunk)), tab((heads, chunk, LANES)), tab((heads, chunk, LANES)),
            tab((heads, 8, LANES)), vec, vec,
        ],
        out_specs=[
            pl.BlockSpec((chunk, v_w), lambda b, c: (b * nc + c, 0)),
            pl.BlockSpec((1, heads, dk, dv), lambda b, c: (b, 0, 0, 0)),
        ],
        out_shape=[
            jax.ShapeDtypeStruct((batch * seq, v_w), BF16),
            jax.ShapeDtypeStruct((batch, heads, dk, dv), F32),
        ],
        scratch_shapes=[pltpu.VMEM((heads, dk, dv), F32)],
        compiler_params=_params("parallel", "arbitrary"),
        name="ret_chunk_scan",
    )(hq, decay, rowscale, w_end, g_chunk,
      gn_w.reshape(1, v_w).astype(F32), gn_b.reshape(1, v_w).astype(F32))
    return y, state


def _column(row):
    n = row.shape[1]
    eye = lax.broadcasted_iota(jnp.int32, (n, n), 0) == lax.broadcasted_iota(jnp.int32, (n, n), 1)
    return jnp.sum(jnp.where(eye, row, 0.0), axis=1, keepdims=True)


def _ret_sample_kernel(gam_ref, h_ref, st_ref, gw_ref, gb_ref, y_ref, ns_ref, *, heads, dk, dv):
    qk_w = heads * dk
    v_w = heads * dv
    for h in range(heads):
        gam = gam_ref[h]
        q = h_ref[0, :, h * dk:(h + 1) * dk]
        k = h_ref[0, :, qk_w + h * dk:qk_w + (h + 1) * dk]
        v = h_ref[0, :, 2 * qk_w + h * dv:2 * qk_w + (h + 1) * dv]
        g = h_ref[0, :, 2 * qk_w + v_w + h * dv:2 * qk_w + v_w + (h + 1) * dv]
        r = st_ref[0, 0, h]
        s = jnp.sum(q * k, axis=-1, keepdims=True)
        o = s * v + jnp.sum(r * _column(q), axis=0, keepdims=True) * gam
        ns_ref[0, 0, h] = r * gam + _column(k) * v
        gw = gw_ref[:, h * dv:(h + 1) * dv]
        gb = gb_ref[:, h * dv:(h + 1) * dv]
        y_ref[0, :, h * dv:(h + 1) * dv] = _group_norm_gate(o, gw, gb, g).astype(y_ref.dtype)


def _ret_sample(hq, state, gn_w, gn_b, *, layer, heads, dk, dv):
    batch = hq.shape[0]
    n = hq.shape[1]
    v_w = heads * dv
    gam = jnp.exp(1.0 * _ret_log_decay(heads))
    y, new_state = pl.pallas_call(
        functools.partial(_ret_sample_kernel, heads=heads, dk=dk, dv=dv),
        grid=(batch,),
        in_specs=[
            pl.BlockSpec(memory_space=pltpu.SMEM),
            pl.BlockSpec((1, 1, n), lambda b: (b, 0, 0)),
            pl.BlockSpec((1, 1, heads, dk, dv), lambda b: (layer, b, 0, 0, 0)),
            pl.BlockSpec((1, v_w), lambda b: (0, 0)),
            pl.BlockSpec((1, v_w), lambda b: (0, 0)),
        ],
        out_specs=[
            pl.BlockSpec((1, 1, v_w), lambda b: (b, 0, 0)),
            pl.BlockSpec((1, 1, heads, dk, dv), lambda b: (0, b, 0, 0, 0)),
        ],
        out_shape=[
            jax.ShapeDtypeStruct((batch, 1, v_w), BF16),
            jax.ShapeDtypeStruct((1,) + state.shape[1:], F32),
        ],
        compiler_params=_params("parallel"),
        name="ret_sample_step",
    )(gam, hq.reshape(batch, 1, n), state,
      gn_w.reshape(1, v_w).astype(F32), gn_b.reshape(1, v_w).astype(F32))
    return y.reshape(batch, v_w), new_state


def _residual_epilogue(j, acc, extra, outs):
    (res_ref,) = extra
    (o_ref,) = outs
    o_ref[...] = res_ref[...] + acc


def _proj_residual(a, w, res):
    m = a.shape[0]
    n = w.shape[1]
    tm = _tile(m, 512)
    tn = _tile(n, 1024)
    blk = pl.BlockSpec((tm, tn), lambda i, j: (i, j))
    (out,) = _matmul(a, w, name="proj_residual", extras=(res,), extra_specs=(blk,), epilogue=_residual_epilogue,
                     out_shapes=[jax.ShapeDtypeStruct((m, n), F32)], out_specs=[blk], tm=tm, tn=tn)
    return out


def _mlp_kernel(x_ref, g_ref, wu_ref, wd_ref, o_ref, xn_ref):
    f = pl.program_id(1)

    @pl.when(f == 0)
    def _():
        x = x_ref[...]
        ms = jnp.mean(x * x, axis=-1, keepdims=True)
        xn_ref[...] = (x * lax.rsqrt(ms + NORM_EPS) * g_ref[...]).astype(xn_ref.dtype)
        o_ref[...] = x

    h = jnp.dot(xn_ref[...], wu_ref[...], preferred_element_type=F32)
    h = jnp.square(jnp.maximum(h, 0.0)).astype(BF16)
    o_ref[...] += jnp.dot(h, wd_ref[...], preferred_element_type=F32)


def _mlp_residual(x, gain, w_up, w_down):
    m, d = x.shape
    ff = w_up.shape[1]
    tm = _tile(m, 512)
    tf = _tile(ff, 1024)
    return pl.pallas_call(
        _mlp_kernel,
        grid=(m // tm, ff // tf),
        in_specs=[
            pl.BlockSpec((tm, d), lambda i, f: (i, 0)),
            pl.BlockSpec((1, d), lambda i, f: (0, 0)),
            pl.BlockSpec((d, tf), lambda i, f: (0, f)),
            pl.BlockSpec((tf, d), lambda i, f: (f, 0)),
        ],
        out_specs=pl.BlockSpec((tm, d), lambda i, f: (i, 0)),
        out_shape=jax.ShapeDtypeStruct((m, d), F32),
        scratch_shapes=[pltpu.VMEM((tm, d), BF16)],
        compiler_params=_params("parallel", "arbitrary"),
        name="mlp_residual",
    )(x, gain.reshape(1, d).astype(F32), w_up, w_down)


def _rmsnorm_kernel(x_ref, g_ref, o_ref):
    x = x_ref[...]
    ms = jnp.mean(x * x, axis=-1, keepdims=True)
    o_ref[...] = x * lax.rsqrt(ms + NORM_EPS) * g_ref[...]


def _rmsnorm(x, gain):
    m, d = x.shape
    tm = _tile(m, 1024)
    return pl.pallas_call(
        _rmsnorm_kernel,
        grid=(m // tm,),
        in_specs=[pl.BlockSpec((tm, d), lambda i: (i, 0)), pl.BlockSpec((1, d), lambda i: (0, 0))],
        out_specs=pl.BlockSpec((tm, d), lambda i: (i, 0)),
        out_shape=jax.ShapeDtypeStruct((m, d), F32),
        compiler_params=_params("parallel"),
        name="final_rmsnorm",
    )(x, gain.reshape(1, d).astype(F32))


def _mla_in_epilogue(j, acc, extra, outs, *, q_lora, kv_lora):
    qg_ref, kg_ref, cos_ref, sa_ref, sb_ref = extra
    cq_ref, ckv_ref, ckvb_ref, kr_ref, krb_ref = outs

    def norm(x, g):
        return x * lax.rsqrt(jnp.mean(x * x, axis=-1, keepdims=True) + NORM_EPS) * g

    cq_ref[...] = norm(acc[:, :q_lora], qg_ref[...]).astype(cq_ref.dtype)
    ckv = norm(acc[:, q_lora:q_lora + kv_lora], kg_ref[...])
    ckv_ref[...] = ckv
    ckvb_ref[...] = ckv.astype(ckvb_ref.dtype)
    kr = _rope64(acc[:, q_lora + kv_lora:], cos_ref[...], sa_ref[...], sb_ref[...])
    kr_ref[...] = kr
    krb_ref[...] = kr.astype(krb_ref.dtype)


def _mla_in_proj(x, gain, w, q_gain, kv_gain, tabs, *, q_lora, kv_lora, seq):
    m = x.shape[0]
    n = w.shape[1]
    tm = _tile(min(m, seq), 512)
    nseq = seq // tm
    tab = pl.BlockSpec((tm, LANES), lambda i, j: (i % nseq, 0))
    vec = lambda width: pl.BlockSpec((1, width), lambda i, j: (0, 0))
    blk = lambda width: pl.BlockSpec((tm, width), lambda i, j: (i, 0))
    return _matmul(
        x, w, name="mla_in_proj", gain=gain,
        extras=(q_gain.reshape(1, q_lora).astype(F32), kv_gain.reshape(1, kv_lora).astype(F32)) + tuple(tabs),
        extra_specs=(vec(q_lora), vec(kv_lora), tab, tab, tab),
        epilogue=functools.partial(_mla_in_epilogue, q_lora=q_lora, kv_lora=kv_lora),
        out_shapes=[jax.ShapeDtypeStruct((m, q_lora), BF16),
                    jax.ShapeDtypeStruct((m, kv_lora), F32),
                    jax.ShapeDtypeStruct((m, kv_lora), BF16),
                    jax.ShapeDtypeStruct((m, LANES), F32),
                    jax.ShapeDtypeStruct((m, LANES), BF16)],
        out_specs=[blk(q_lora), blk(kv_lora), blk(kv_lora), blk(LANES), blk(LANES)],
        tm=tm, tn=n)


def _mla_q_epilogue(j, acc, extra, outs, *, n_nope, scale):
    cos_ref, sa_ref, sb_ref = extra
    (o_ref,) = outs

    @pl.when(j < n_nope)
    def _():
        o_ref[...] = (acc * scale).astype(o_ref.dtype)

    @pl.when(j >= n_nope)
    def _():
        cos = cos_ref[...]
        sa = sa_ref[...]
        sb = sb_ref[...]
        for h in range(acc.shape[1] // LANES):
            x = acc[:, h * LANES:(h + 1) * LANES]
            o_ref[:, h * LANES:(h + 1) * LANES] = (_rope64(x, cos, sa, sb) * scale).astype(o_ref.dtype)


def _mla_q_proj(cq, w, tabs, *, nope_w, scale, seq, out_dtype):
    m = cq.shape[0]
    n = w.shape[1]
    tm = _tile(min(m, seq), 1024)
    tn = _tile(nope_w, 1024)
    nseq = seq // tm
    tab = pl.BlockSpec((tm, LANES), lambda i, j: (i % nseq, 0))
    (out,) = _matmul(
        cq, w, name="mla_q_proj", extras=tuple(tabs), extra_specs=(tab, tab, tab),
        epilogue=functools.partial(_mla_q_epilogue, n_nope=nope_w // tn, scale=scale),
        out_shapes=[jax.ShapeDtypeStruct((m, n), out_dtype)],
        out_specs=[pl.BlockSpec((tm, tn), lambda i, j: (i, j))],
        tm=tm, tn=tn)
    return out


def _cast_epilogue(j, acc, extra, outs):
    (o_ref,) = outs
    o_ref[...] = acc.astype(o_ref.dtype)


def _plain_matmul(a, w, out_dtype):
    m = a.shape[0]
    n = w.shape[1]
    tm = _tile(m, 1024)
    tn = _tile(n, 1024)
    (out,) = _matmul(a, w, name="mla_k_expand", epilogue=_cast_epilogue,
                     out_shapes=[jax.ShapeDtypeStruct((m, n), out_dtype)],
                     out_specs=[pl.BlockSpec((tm, tn), lambda i, j: (i, j))], tm=tm, tn=tn)
    return out


FLASH_TILE = 512
FLASH_HEADS_PER_STEP = 2


def _vt_kernel(x_ref, wt_ref, o_ref):
    o_ref[0] = _nt_dot(wt_ref[...], x_ref[...]).astype(o_ref.dtype)


def _mla_v_expand_t(ckvb, w_uv_t, tq):
    m, c = ckvb.shape
    n = w_uv_t.shape[0]
    tn = _tile(n, 1024)
    return pl.pallas_call(
        _vt_kernel,
        grid=(m // tq, n // tn),
        in_specs=[pl.BlockSpec((tq, c), lambda i, j: (i, 0)), pl.BlockSpec((tn, c), lambda i, j: (j, 0))],
        out_specs=pl.BlockSpec((1, tn, tq), lambda i, j: (i, j, 0)),
        out_shape=jax.ShapeDtypeStruct((m // tq, n, tq), BF16),
        compiler_params=_params("parallel", "arbitrary"),
        name="mla_v_expand_t",
    )(ckvb, w_uv_t)


def _flash_kernel(qn_ref, qr_ref, kn_ref, kr_ref, vt_ref, o_ref, *, tq, hps):
    qi = pl.program_id(2)
    dv = vt_ref.shape[1] // hps
    qs = [jnp.concatenate([qn_ref[:, h * LANES:(h + 1) * LANES], qr_ref[:, h * LANES:(h + 1) * LANES]], axis=1)
          for h in range(hps)]

    def scores(h, start):
        k = jnp.concatenate([kn_ref[pl.ds(start, tq), h * LANES:(h + 1) * LANES], kr_ref[pl.ds(start, tq), :]],
                            axis=1)
        return _nt_dot(k, qs[h])

    def update(carry, s, h, kb):
        m, l, acc = carry
        m_new = jnp.maximum(m, jnp.max(s, axis=0, keepdims=True))
        corr = jnp.exp2(m - m_new)
        p = jnp.exp2(s - m_new)
        l = l * corr + jnp.sum(p, axis=0, keepdims=True)
        vt = vt_ref[kb, h * dv:(h + 1) * dv, :]
        acc = acc * corr + jnp.dot(vt, p.astype(BF16), preferred_element_type=F32)
        return m_new, l, acc

    def body(kb, carries):
        start = pl.multiple_of(kb * tq, tq)
        return tuple(update(carries[h], scores(h, start), h, kb) for h in range(hps))

    init = tuple((jnp.full((1, tq), NEG, F32), jnp.zeros((1, tq), F32), jnp.zeros((dv, tq), F32))
                 for _ in range(hps))
    carries = lax.fori_loop(0, qi, body, init)
    start = pl.multiple_of(qi * tq, tq)
    causal = lax.broadcasted_iota(jnp.int32, (tq, tq), 0) <= lax.broadcasted_iota(jnp.int32, (tq, tq), 1)
    for h in range(hps):
        _, l, acc = update(carries[h], jnp.where(causal, scores(h, start), NEG), h, qi)
        o_ref[:, h * dv:(h + 1) * dv] = (acc / l).T.astype(o_ref.dtype)


def _mla_prompt_attention(q, kn, krb, vt, *, batch, seq, heads):
    m = q.shape[0]
    tq = vt.shape[2]
    nq = seq // tq
    hps = FLASH_HEADS_PER_STEP if heads % FLASH_HEADS_PER_STEP == 0 else 1
    w = hps * LANES
    ng = heads // hps
    qblk = lambda off: pl.BlockSpec((tq, w), lambda b, h, i: (b * nq + i, off + h))
    return pl.pallas_call(
        functools.partial(_flash_kernel, tq=tq, hps=hps),
        grid=(batch, ng, nq),
        in_specs=[qblk(0), qblk(ng),
                  pl.BlockSpec((seq, w), lambda b, h, i: (b, h)),
                  pl.BlockSpec((seq, LANES), lambda b, h, i: (b, 0)),
                  pl.BlockSpec((nq, w, tq), lambda b, h, i: (b, h, 0))],
        out_specs=pl.BlockSpec((tq, w), lambda b, h, i: (b * nq + i, h)),
        out_shape=jax.ShapeDtypeStruct((m, heads * LANES), BF16),
        compiler_params=_params("parallel", "parallel", "arbitrary"),
        name="mla_prompt_flash",
    )(q, q, kn, krb, vt)


def _head_nt_kernel(a_ref, w_ref, o_ref):
    o_ref[0] = _nt_dot(a_ref[...], w_ref[...]).astype(o_ref.dtype)


def _absorb_uk(qn, w_uk2, *, heads):
    b = qn.shape[0]
    c = w_uk2.shape[0]
    return pl.pallas_call(
        _head_nt_kernel,
        grid=(heads,),
        in_specs=[pl.BlockSpec((b, LANES), lambda h: (0, h)), pl.BlockSpec((c, LANES), lambda h: (0, h))],
        out_specs=pl.BlockSpec((1, b, c), lambda h: (h, 0, 0)),
        out_shape=jax.ShapeDtypeStruct((heads, b, c), F32),
        compiler_params=_params("parallel"),
        name="mla_absorb_uk",
    )(qn, w_uk2)


def _head_nn_kernel(a_ref, w_ref, o_ref):
    o_ref[...] = jnp.dot(a_ref[0], w_ref[...], preferred_element_type=F32).astype(o_ref.dtype)


def _expand_uv(o_lat, w_uv2, *, heads):
    b = o_lat.shape[1]
    c = o_lat.shape[2]
    return pl.pallas_call(
        _head_nn_kernel,
        grid=(heads,),
        in_specs=[pl.BlockSpec((1, b, c), lambda h: (h, 0, 0)), pl.BlockSpec((c, LANES), lambda h: (0, h))],
        out_specs=pl.BlockSpec((b, LANES), lambda h: (0, h)),
        out_shape=jax.ShapeDtypeStruct((b, heads * LANES), BF16),
        compiler_params=_params("parallel"),
        name="mla_expand_uv",
    )(o_lat, w_uv2)


def _paged_kernel(pt_ref, ql_ref, qr_ref, cn_ref, kn_ref, *rest, pps, rope_dim):
    ckv_refs = rest[:pps]
    kr_refs = rest[pps:2 * pps]
    o_ref, m_ref, l_ref, acc_ref = rest[2 * pps:]
    g = pl.program_id(1)

    @pl.when(g == 0)
    def _():
        m_ref[...] = jnp.full_like(m_ref, NEG)
        l_ref[...] = jnp.zeros_like(l_ref)
        acc_ref[...] = jnp.zeros_like(acc_ref)

    ql = ql_ref[0]
    qr = qr_ref[0][:, :rope_dim]
    qlb = ql.astype(BF16)
    qrb = qr.astype(BF16)
    pages = [r[0, 0].astype(BF16) for r in ckv_refs]
    s = jnp.concatenate(
        [_nt_dot(qlb, pages[i])
         + jnp.dot(qrb, kr_refs[i][0, 0].astype(BF16), preferred_element_type=F32)
         for i in range(pps)], axis=1)
    m = m_ref[...]
    m_new = jnp.maximum(m, jnp.max(s, axis=-1, keepdims=True))
    corr = jnp.exp(m - m_new)
    p = jnp.exp(s - m_new)
    l_new = l_ref[...] * corr + jnp.sum(p, axis=-1, keepdims=True)
    pb = p.astype(BF16)
    page = pages[0].shape[0]
    pv = jnp.dot(pb[:, :page], pages[0], preferred_element_type=F32)
    for i in range(1, pps):
        pv = pv + jnp.dot(pb[:, i * page:(i + 1) * page], pages[i], preferred_element_type=F32)
    acc_new = acc_ref[...] * corr + pv
    m_ref[...] = m_new
    l_ref[...] = l_new
    acc_ref[...] = acc_new

    @pl.when(g == pl.num_programs(1) - 1)
    def _():
        cn = cn_ref[0]
        s_new = (jnp.sum(ql * cn, axis=-1, keepdims=True)
                 + jnp.sum(qr * kn_ref[0][:, :rope_dim], axis=-1, keepdims=True))
        m_fin = jnp.maximum(m_new, s_new)
        c_fin = jnp.exp(m_new - m_fin)
        p_new = jnp.exp(s_new - m_fin)
        l_fin = l_new * c_fin + p_new
        o_ref[0] = (acc_new * c_fin + p_new * cn) / l_fin


PAGES_PER_STEP = 32


def _mla_paged_attention(q_lat, qr, ckv_new, kr_new, cache_ckv, cache_kr_t, page_table, *, layer):
    b, heads, c = q_lat.shape
    n_pages = page_table.shape[1]
    page = cache_ckv.shape[2]
    rope_dim = cache_kr_t.shape[2]
    pps = _tile(n_pages, PAGES_PER_STEP)
    per_b = lambda width: pl.BlockSpec((1, heads, width), lambda i, g, pt: (i, 0, 0))
    new_b = lambda width: pl.BlockSpec((1, 1, width), lambda i, g, pt: (i, 0, 0))

    def page_spec(rows, width, k):
        return pl.BlockSpec((1, 1, rows, width), lambda i, g, pt: (layer, pt[i, g * pps + k], 0, 0))

    grid_spec = pltpu.PrefetchScalarGridSpec(
        num_scalar_prefetch=1,
        grid=(b, n_pages // pps),
        in_specs=[per_b(c), per_b(LANES), new_b(c), new_b(LANES)]
        + [page_spec(page, c, k) for k in range(pps)] + [page_spec(rope_dim, page, k) for k in range(pps)],
        out_specs=per_b(c),
        scratch_shapes=[pltpu.VMEM((heads, 1), F32), pltpu.VMEM((heads, 1), F32), pltpu.VMEM((heads, c), F32)],
    )
    return pl.pallas_call(
        functools.partial(_paged_kernel, pps=pps, rope_dim=rope_dim),
        grid_spec=grid_spec,
        out_shape=jax.ShapeDtypeStruct((b, heads, c), F32),
        compiler_params=_params("parallel", "arbitrary"),
        name="mla_paged_attention",
    )(page_table, q_lat, qr, ckv_new.reshape(b, 1, c), kr_new.reshape(b, 1, LANES),
      *([cache_ckv] * pps), *([cache_kr_t] * pps))


def _concat_layers(parts):
    return parts[0] if len(parts) == 1 else jnp.concatenate(parts, axis=0)


def kernel(x_prompt, x_sample, state_ret, cache_kv_latent, cache_k_rope, page_table, norm_mix, norm_mlp, norm_final, w_ret_in, ret_gn_w, ret_gn_b, w_ret_out, w_mla_in, mla_q_norm, mla_kv_norm, w_mla_uq, w_mla_uk, w_mla_uv, w_mla_out, w_up, w_down):
    bp, seq, d = x_prompt.shape
    bs, ls, _ = x_sample.shape
    assert ls == 1, "the sample group is a single new token per sequence"
    depth = norm_mix.shape[0]
    ret_heads, ret_dk, ret_dv = state_ret.shape[2:]
    qk_w = ret_heads * ret_dk
    v_w = ret_heads * ret_dv
    kv_lora, mla_heads, nope = w_mla_uk.shape[1:]
    v_dim = w_mla_uv.shape[3]
    q_lora = mla_q_norm.shape[1]
    rope_dim = cache_k_rope.shape[3]
    page = cache_kv_latent.shape[2]
    assert ret_dk == 2 * LANES and nope == LANES and v_dim == LANES and rope_dim * 2 == LANES
    past = page_table.shape[1] * page
    scale = float((nope + rope_dim) ** -0.5)

    pos_p = jnp.arange(seq, dtype=F32)
    pos_s = jnp.broadcast_to(past + jnp.arange(ls, dtype=F32), (bs,))
    ret_tab_p = _rope_cos_sin(pos_p, ret_dk // 2)
    ret_tab_s = _rope_cos_sin(pos_s, ret_dk // 2)
    mla_tab_p = _rope64_tables(pos_p)
    mla_tab_s = _rope64_tables(pos_s)

    xp = x_prompt.reshape(bp * seq, d)
    xs = x_sample.reshape(bs, d)
    ret_p, ret_s, ckv_p, kr_p, ckv_s, kr_s = [], [], [], [], [], []
    for i in range(depth):
        j = i // 2
        if i % 2 == 0:
            w_in = w_ret_in[j].astype(BF16)
            w_out = w_ret_out[j].astype(BF16)
            hq = _ret_in_proj(xp, norm_mix[i], w_in, *ret_tab_p, qk_w=qk_w, v_w=v_w, seq=seq, out_dtype=BF16)
            y, r_p = _ret_prompt(hq, ret_gn_w[j], ret_gn_b[j], batch=bp, seq=seq,
                                 heads=ret_heads, dk=ret_dk, dv=ret_dv)
            xp = _proj_residual(y, w_out, xp)
            hq = _ret_in_proj(xs, norm_mix[i], w_in, *ret_tab_s, qk_w=qk_w, v_w=v_w, seq=bs, out_dtype=F32)
            y, r_s = _ret_sample(hq, state_ret, ret_gn_w[j], ret_gn_b[j], layer=j,
                                 heads=ret_heads, dk=ret_dk, dv=ret_dv)
            xs = _proj_residual(y, w_out, xs)
            ret_p.append(r_p[None])
            ret_s.append(r_s)
        else:
            w_in = jnp.pad(w_mla_in[j], ((0, 0), (0, LANES - rope_dim))).astype(BF16)
            w_uq = w_mla_uq[j].reshape(q_lora, mla_heads, nope + rope_dim)
            w_uq = jnp.concatenate(
                [w_uq[:, :, :nope].reshape(q_lora, mla_heads * nope),
                 jnp.pad(w_uq[:, :, nope:], ((0, 0), (0, 0), (0, LANES - rope_dim))).reshape(q_lora, mla_heads * LANES)],
                axis=1).astype(BF16)
            w_uk2 = w_mla_uk[j].reshape(kv_lora, mla_heads * nope).astype(BF16)
            w_uv2 = w_mla_uv[j].reshape(kv_lora, mla_heads * v_dim).astype(BF16)
            w_out = w_mla_out[j].astype(BF16)
            nope_w = mla_heads * nope

            cq, ckv, ckvb, kr, krb = _mla_in_proj(xp, norm_mix[i], w_in, mla_q_norm[j], mla_kv_norm[j], mla_tab_p,
                                                  q_lora=q_lora, kv_lora=kv_lora, seq=seq)
            q = _mla_q_proj(cq, w_uq, mla_tab_p, nope_w=nope_w, scale=scale * LOG2_E, seq=seq, out_dtype=BF16)
            kn = _plain_matmul(ckvb, w_uk2, BF16)
            vt = _mla_v_expand_t(ckvb, w_uv2.T, _tile(seq, FLASH_TILE))
            o = _mla_prompt_attention(q, kn, krb, vt, batch=bp, seq=seq, heads=mla_heads)
            xp = _proj_residual(o, w_out, xp)
            ckv_p.append(ckv.reshape(1, bp, seq // page, page, kv_lora))
            kr_p.append(kr[:, :rope_dim].reshape(1, bp, seq // page, page, rope_dim))

            cq, ckv, ckvb, kr, krb = _mla_in_proj(xs, norm_mix[i], w_in, mla_q_norm[j], mla_kv_norm[j], mla_tab_s,
                                                  q_lora=q_lora, kv_lora=kv_lora, seq=bs)
            q = _mla_q_proj(cq, w_uq, mla_tab_s, nope_w=nope_w, scale=scale, seq=bs, out_dtype=F32)
            q_lat = _absorb_uk(q[:, :nope_w].astype(BF16), w_uk2, heads=mla_heads)
            o_lat = _mla_paged_attention(
                q_lat.transpose(1, 0, 2), q[:, nope_w:].reshape(bs, mla_heads, LANES), ckv, kr,
                cache_kv_latent, jnp.swapaxes(cache_k_rope, 2, 3), page_table, layer=j)
            o = _expand_uv(o_lat.transpose(1, 0, 2).astype(BF16), w_uv2, heads=mla_heads)
            xs = _proj_residual(o, w_out, xs)
            ckv_s.append(ckv.reshape(1, bs, ls, kv_lora))
            kr_s.append(kr[:, :rope_dim].reshape(1, bs, ls, rope_dim))
        w_u = w_up[i].astype(BF16)
        w_d = w_down[i].astype(BF16)
        xp = _mlp_residual(xp, norm_mlp[i], w_u, w_d)
        xs = _mlp_residual(xs, norm_mlp[i], w_u, w_d)
    y_prompt = _rmsnorm(xp, norm_final).reshape(bp, seq, d)
    y_sample = _rmsnorm(xs, norm_final).reshape(bs, ls, d)
    return (y_prompt, y_sample) + tuple(_concat_layers(t) for t in (ret_p, ret_s, ckv_p, kr_p, ckv_s, kr_s))
```

```python
import functools

import jax
import jax.numpy as jnp
from jax import lax
from jax.experimental import pallas as pl
from jax.experimental.pallas import tpu as pltpu

ROPE_BASE = 10000.0
NORM_EPS = 1e-6
GN_EPS = 1e-5
NEG = -1e30
RET_CHUNK = 128
LOG2_E = 1.4426950408889634

LANES = 128
BF16_SUBLANES = 16
V7X_VMEM_BYTES = 64 * 1024 * 1024
VMEM_LIMIT = 52 * 1024 * 1024

F32 = jnp.float32
BF16 = jnp.bfloat16


def _params(*sem):
    return pltpu.CompilerParams(dimension_semantics=sem, vmem_limit_bytes=VMEM_LIMIT)


def _tile(n, pref):
    if n <= pref:
        return n
    t = pref
    while n % t:
        t //= 2
    return t


def _nt_dot(a, b):
    return lax.dot_general(a, b, (((1,), (1,)), ((), ())), preferred_element_type=F32)


def _tn_dot(a, b):
    return lax.dot_general(a, b, (((0,), (0,)), ((), ())), preferred_element_type=F32)


def _rope_cos_sin(pos, d2):
    inv = ROPE_BASE ** (-jnp.arange(d2, dtype=F32) / d2)
    ang = pos[:, None] * inv[None, :]
    return jnp.cos(ang), jnp.sin(ang)


def _rope64_tables(pos):
    cos, sin = _rope_cos_sin(pos, 32)
    z = jnp.zeros_like(sin)
    cos_t = jnp.concatenate([cos, cos, cos, cos], axis=-1)
    sa = jnp.concatenate([-sin, z, -sin, z], axis=-1)
    sb = jnp.concatenate([z, sin, z, sin], axis=-1)
    return cos_t, sa, sb


def _rope64(x, cos_t, sa, sb):
    return x * cos_t + pltpu.roll(x, 96, 1) * sa + pltpu.roll(x, 32, 1) * sb


def _mm_kernel(*refs, has_norm, n_extra, epilogue):
    if has_norm:
        x_ref, g_ref, w_ref = refs[:3]
        rest = refs[3:]
        xn_ref = rest[-1]
        rest = rest[:-1]

        @pl.when(pl.program_id(1) == 0)
        def _():
            x = x_ref[...]
            ms = jnp.mean(x * x, axis=-1, keepdims=True)
            xn_ref[...] = (x * lax.rsqrt(ms + NORM_EPS) * g_ref[...]).astype(xn_ref.dtype)

        lhs = xn_ref[...]
    else:
        x_ref, w_ref = refs[:2]
        rest = refs[2:]
        lhs = x_ref[...]
    extra = rest[:n_extra]
    outs = rest[n_extra:]
    product = lambda: jnp.dot(lhs, w_ref[...], preferred_element_type=F32)
    epilogue(pl.program_id(1), product, extra, outs)


def _matmul(x, w, *, name, gain=None, extras=(), extra_specs=(), epilogue, out_shapes, out_specs, tm, tn):
    m, k = x.shape
    n = w.shape[1]
    has_norm = gain is not None
    in_specs = [pl.BlockSpec((tm, k), lambda i, j: (i, 0))]
    args = [x]
    if has_norm:
        in_specs.append(pl.BlockSpec((1, k), lambda i, j: (0, 0)))
        args.append(gain.reshape(1, k).astype(F32))
    in_specs.append(pl.BlockSpec((k, tn), lambda i, j: (0, j)))
    args.append(w)
    in_specs += list(extra_specs)
    args += list(extras)
    scratch = [pltpu.VMEM((tm, k), BF16)] if has_norm else []
    return pl.pallas_call(
        functools.partial(_mm_kernel, has_norm=has_norm, n_extra=len(extras), epilogue=epilogue),
        grid=(m // tm, n // tn),
        in_specs=in_specs,
        out_specs=out_specs,
        out_shape=out_shapes,
        scratch_shapes=scratch,
        compiler_params=_params("parallel", "arbitrary"),
        name=name,
    )(*args)


def _ret_in_epilogue(j, product, extra, outs, *, tn, qk_w, v_w):
    cos_ref, sin_ref = extra
    (o_ref,) = outs
    n_qk = (2 * qk_w) // tn
    n_v = v_w // tn

    @pl.when(j < n_qk)
    def _():
        acc = product()
        cos = cos_ref[...]
        sin = sin_ref[...]
        scale = jnp.where(j >= qk_w // tn, 0.0625, 1.0).astype(F32)
        for h in range(tn // 256):
            x1 = acc[:, h * 256:h * 256 + 128]
            x2 = acc[:, h * 256 + 128:(h + 1) * 256]
            o_ref[:, h * 256:h * 256 + 128] = ((x1 * cos - x2 * sin) * scale).astype(o_ref.dtype)
            o_ref[:, h * 256 + 128:(h + 1) * 256] = ((x2 * cos + x1 * sin) * scale).astype(o_ref.dtype)

    @pl.when(jnp.logical_and(j >= n_qk, j < n_qk + n_v))
    def _():
        o_ref[...] = product().astype(o_ref.dtype)

    @pl.when(j >= n_qk + n_v)
    def _():
        acc = product()
        o_ref[...] = (acc * jax.nn.sigmoid(acc)).astype(o_ref.dtype)


def _ret_in_proj(x, gain, w, cos, sin, *, qk_w, v_w, seq, out_dtype):
    m = x.shape[0]
    n = w.shape[1]
    tm = _tile(min(m, seq), 1024)
    tn = _tile(qk_w, 1024)
    nseq = seq // tm
    tab = pl.BlockSpec((tm, LANES), lambda i, j: (i % nseq, 0))
    (out,) = _matmul(
        x, w, name="ret_in_proj", gain=gain, extras=(cos, sin), extra_specs=(tab, tab),
        epilogue=functools.partial(_ret_in_epilogue, tn=tn, qk_w=qk_w, v_w=v_w),
        out_shapes=[jax.ShapeDtypeStruct((m, n), out_dtype)],
        out_specs=[pl.BlockSpec((tm, tn), lambda i, j: (i, j))],
        tm=tm, tn=tn)
    return out


def _ret_log_decay(heads):
    return jnp.log1p(-jnp.exp2(-5.0 - jnp.arange(heads, dtype=F32)))


def _group_norm_gate(o, gw, gb, gate):
    mu = jnp.mean(o, axis=-1, keepdims=True)
    d = o - mu
    var = jnp.mean(d * d, axis=-1, keepdims=True)
    return (d * lax.rsqrt(var + GN_EPS) * gw + gb) * gate


def _ret_chunk_kernel(h_ref, dec_ref, rs_ref, we_ref, gc_ref, gw_ref, gb_ref, y_ref, st_ref, r_ref,
                      *, heads, dk, dv):
    c = pl.program_id(1)
    qk_w = heads * dk
    v_w = heads * dv

    @pl.when(c == 0)
    def _():
        r_ref[...] = jnp.zeros_like(r_ref)

    for h in range(heads):
        q = h_ref[:, h * dk:(h + 1) * dk]
        k = h_ref[:, qk_w + h * dk:qk_w + (h + 1) * dk]
        v = h_ref[:, 2 * qk_w + h * dv:2 * qk_w + (h + 1) * dv]
        g = h_ref[:, 2 * qk_w + v_w + h * dv:2 * qk_w + v_w + (h + 1) * dv]
        r = r_ref[h]
        s = (_nt_dot(q, k) * dec_ref[h]).astype(BF16)
        rs = jnp.concatenate([rs_ref[h]] * (dv // LANES), axis=1)
        o = jnp.dot(s, v, preferred_element_type=F32)
        o = o + jnp.dot(q, r.astype(BF16), preferred_element_type=F32) * rs
        we = jnp.concatenate([we_ref[h]] * (dk // LANES), axis=1)
        kw = (k.astype(F32) * we).astype(BF16)
        gc = jnp.concatenate([gc_ref[h, 0:1, :]] * (dv // LANES), axis=1)
        r_ref[h] = r * gc + _tn_dot(kw, v)
        gw = gw_ref[:, h * dv:(h + 1) * dv]
        gb = gb_ref[:, h * dv:(h + 1) * dv]
        y_ref[:, h * dv:(h + 1) * dv] = _group_norm_gate(o, gw, gb, g.astype(F32)).astype(y_ref.dtype)

    @pl.when(c == pl.num_programs(1) - 1)
    def _():
        st_ref[0] = r_ref[...]


def _ret_prompt(hq, gn_w, gn_b, *, batch, seq, heads, dk, dv):
    chunk = RET_CHUNK
    nc = seq // chunk
    qk_w = heads * dk
    v_w = heads * dv
    log_g = _ret_log_decay(heads)
    n = jnp.arange(chunk, dtype=F32)
    diff = n[:, None] - n[None, :]
    causal = diff >= 0
    decay = jnp.where(causal[None], jnp.exp(jnp.where(causal, diff, 0.0)[None] * log_g[:, None, None]), 0.0)
    ones = jnp.ones((1, 1, LANES), F32)
    rowscale = jnp.exp((n + 1.0)[None, :] * log_g[:, None])[:, :, None] * ones
    w_end = jnp.exp((chunk - 1.0 - n)[None, :] * log_g[:, None])[:, :, None] * ones
    g_chunk = jnp.exp(chunk * log_g)[:, None, None] * jnp.ones((1, 8, LANES), F32)

    tab = lambda shape: pl.BlockSpec(shape, lambda b, c: (0, 0, 0))
    vec = pl.BlockSpec((1, v_w), lambda b, c: (0, 0))
    y, state = pl.pallas_call(
        functools.partial(_ret_chunk_kernel, heads=heads, dk=dk, dv=dv),
        grid=(batch, nc),
        in_specs=[
            pl.BlockSpec((chunk, hq.shape[1]), lambda b, c: (b * nc + c, 0)),
            tab((heads, chunk, chunk)), tab((heads, chunk, LANES)), tab((heads, chunk, LANES)),
            tab((heads, 8, LANES)), vec, vec,
        ],
        out_specs=[
            pl.BlockSpec((chunk, v_w), lambda b, c: (b * nc + c, 0)),
            pl.BlockSpec((1, heads, dk, dv), lambda b, c: (b, 0, 0, 0)),
        ],
        out_shape=[
            jax.ShapeDtypeStruct((batch * seq, v_w), BF16),
            jax.ShapeDtypeStruct((batch, heads, dk, dv), F32),
        ],
        scratch_shapes=[pltpu.VMEM((heads, dk, dv), F32)],
        compiler_params=_params("parallel", "arbitrary"),
        name="ret_chunk_scan",
    )(hq, decay, rowscale, w_end, g_chunk,
      gn_w.reshape(1, v_w).astype(F32), gn_b.reshape(1, v_w).astype(F32))
    return y, state


def _column(row):
    n = row.shape[1]
    eye = lax.broadcasted_iota(jnp.int32, (n, n), 0) == lax.broadcasted_iota(jnp.int32, (n, n), 1)
    return jnp.sum(jnp.where(eye, row, 0.0), axis=1, keepdims=True)


def _ret_sample_kernel(gam_ref, h_ref, st_ref, gw_ref, gb_ref, y_ref, ns_ref, *, heads, dk, dv):
    qk_w = heads * dk
    v_w = heads * dv
    for h in range(heads):
        gam = gam_ref[h]
        q = h_ref[0, :, h * dk:(h + 1) * dk]
        k = h_ref[0, :, qk_w + h * dk:qk_w + (h + 1) * dk]
        v = h_ref[0, :, 2 * qk_w + h * dv:2 * qk_w + (h + 1) * dv]
        g = h_ref[0, :, 2 * qk_w + v_w + h * dv:2 * qk_w + v_w + (h + 1) * dv]
        r = st_ref[0, 0, h]
        s = jnp.sum(q * k, axis=-1, keepdims=True)
        o = s * v + jnp.sum(r * _column(q), axis=0, keepdims=True) * gam
        ns_ref[0, 0, h] = r * gam + _column(k) * v
        gw = gw_ref[:, h * dv:(h + 1) * dv]
        gb = gb_ref[:, h * dv:(h + 1) * dv]
        y_ref[0, :, h * dv:(h + 1) * dv] = _group_norm_gate(o, gw, gb, g).astype(y_ref.dtype)


def _ret_sample(hq, state, gn_w, gn_b, *, layer, heads, dk, dv):
    batch = hq.shape[0]
    n = hq.shape[1]
    v_w = heads * dv
    gam = jnp.exp(1.0 * _ret_log_decay(heads))
    y, new_state = pl.pallas_call(
        functools.partial(_ret_sample_kernel, heads=heads, dk=dk, dv=dv),
        grid=(batch,),
        in_specs=[
            pl.BlockSpec(memory_space=pltpu.SMEM),
            pl.BlockSpec((1, 1, n), lambda b: (b, 0, 0)),
            pl.BlockSpec((1, 1, heads, dk, dv), lambda b: (layer, b, 0, 0, 0)),
            pl.BlockSpec((1, v_w), lambda b: (0, 0)),
            pl.BlockSpec((1, v_w), lambda b: (0, 0)),
        ],
        out_specs=[
            pl.BlockSpec((1, 1, v_w), lambda b: (b, 0, 0)),
            pl.BlockSpec((1, 1, heads, dk, dv), lambda b: (0, b, 0, 0, 0)),
        ],
        out_shape=[
            jax.ShapeDtypeStruct((batch, 1, v_w), BF16),
            jax.ShapeDtypeStruct((1,) + state.shape[1:], F32),
        ],
        compiler_params=_params("parallel"),
        name="ret_sample_step",
    )(gam, hq.reshape(batch, 1, n), state,
      gn_w.reshape(1, v_w).astype(F32), gn_b.reshape(1, v_w).astype(F32))
    return y.reshape(batch, v_w), new_state


def _residual_epilogue(j, product, extra, outs):
    (res_ref,) = extra
    (o_ref,) = outs
    o_ref[...] = res_ref[...] + product()


def _proj_residual(a, w, res):
    m = a.shape[0]
    n = w.shape[1]
    tm = _tile(m, 512)
    tn = _tile(n, 1024)
    blk = pl.BlockSpec((tm, tn), lambda i, j: (i, j))
    (out,) = _matmul(a, w, name="proj_residual", extras=(res,), extra_specs=(blk,), epilogue=_residual_epilogue,
                     out_shapes=[jax.ShapeDtypeStruct((m, n), F32)], out_specs=[blk], tm=tm, tn=tn)
    return out


def _mlp_kernel(x_ref, g_ref, wu_ref, wd_ref, *rest, final_norm):
    if final_norm:
        fg_ref, o_ref, xn_ref = rest
    else:
        o_ref, xn_ref = rest
    f = pl.program_id(1)

    @pl.when(f == 0)
    def _():
        x = x_ref[...]
        ms = jnp.mean(x * x, axis=-1, keepdims=True)
        xn_ref[...] = (x * lax.rsqrt(ms + NORM_EPS) * g_ref[...]).astype(xn_ref.dtype)
        o_ref[...] = x

    h = jnp.dot(xn_ref[...], wu_ref[...], preferred_element_type=F32)
    h = jnp.square(jnp.maximum(h, 0.0)).astype(BF16)
    o_ref[...] += jnp.dot(h, wd_ref[...], preferred_element_type=F32)

    if final_norm:
        @pl.when(f == pl.num_programs(1) - 1)
        def _():
            y = o_ref[...]
            ms = jnp.mean(y * y, axis=-1, keepdims=True)
            o_ref[...] = y * lax.rsqrt(ms + NORM_EPS) * fg_ref[...]


def _mlp_residual(x, gain, w_up, w_down, final_gain=None):
    m, d = x.shape
    ff = w_up.shape[1]
    tm = _tile(m, 512)
    tf = _tile(ff, 1024)
    final_norm = final_gain is not None
    vec = pl.BlockSpec((1, d), lambda i, f: (0, 0))
    in_specs = [
        pl.BlockSpec((tm, d), lambda i, f: (i, 0)),
        vec,
        pl.BlockSpec((d, tf), lambda i, f: (0, f)),
        pl.BlockSpec((tf, d), lambda i, f: (f, 0)),
    ]
    args = [x, gain.reshape(1, d).astype(F32), w_up, w_down]
    if final_norm:
        in_specs.append(vec)
        args.append(final_gain.reshape(1, d).astype(F32))
    return pl.pallas_call(
        functools.partial(_mlp_kernel, final_norm=final_norm),
        grid=(m // tm, ff // tf),
        in_specs=in_specs,
        out_specs=pl.BlockSpec((tm, d), lambda i, f: (i, 0)),
        out_shape=jax.ShapeDtypeStruct((m, d), F32),
        scratch_shapes=[pltpu.VMEM((tm, d), BF16)],
        compiler_params=_params("parallel", "arbitrary"),
        name="mlp_residual",
    )(*args)


def _mla_in_epilogue(j, product, extra, outs, *, q_lora, kv_lora):
    qg_ref, kg_ref, cos_ref, sa_ref, sb_ref = extra
    cq_ref, ckv_ref, ckvb_ref, kr_ref, krb_ref = outs
    acc = product()

    def norm(x, g):
        return x * lax.rsqrt(jnp.mean(x * x, axis=-1, keepdims=True) + NORM_EPS) * g

    cq_ref[...] = norm(acc[:, :q_lora], qg_ref[...]).astype(cq_ref.dtype)
    ckv = norm(acc[:, q_lora:q_lora + kv_lora], kg_ref[...])
    ckv_ref[...] = ckv
    ckvb_ref[...] = ckv.astype(ckvb_ref.dtype)
    kr = _rope64(acc[:, q_lora + kv_lora:], cos_ref[...], sa_ref[...], sb_ref[...])
    kr_ref[...] = kr
    krb_ref[...] = kr.astype(krb_ref.dtype)


def _mla_in_proj(x, gain, w, q_gain, kv_gain, tabs, *, q_lora, kv_lora, seq):
    m = x.shape[0]
    n = w.shape[1]
    tm = _tile(min(m, seq), 512)
    nseq = seq // tm
    tab = pl.BlockSpec((tm, LANES), lambda i, j: (i % nseq, 0))
    vec = lambda width: pl.BlockSpec((1, width), lambda i, j: (0, 0))
    blk = lambda width: pl.BlockSpec((tm, width), lambda i, j: (i, 0))
    return _matmul(
        x, w, name="mla_in_proj", gain=gain,
        extras=(q_gain.reshape(1, q_lora).astype(F32), kv_gain.reshape(1, kv_lora).astype(F32)) + tuple(tabs),
        extra_specs=(vec(q_lora), vec(kv_lora), tab, tab, tab),
        epilogue=functools.partial(_mla_in_epilogue, q_lora=q_lora, kv_lora=kv_lora),
        out_shapes=[jax.ShapeDtypeStruct((m, q_lora), BF16),
                    jax.ShapeDtypeStruct((m, kv_lora), F32),
                    jax.ShapeDtypeStruct((m, kv_lora), BF16),
                    jax.ShapeDtypeStruct((m, LANES), F32),
                    jax.ShapeDtypeStruct((m, LANES), BF16)],
        out_specs=[blk(q_lora), blk(kv_lora), blk(kv_lora), blk(LANES), blk(LANES)],
        tm=tm, tn=n)


def _mla_q_epilogue(j, product, extra, outs, *, n_nope, scale):
    cos_ref, sa_ref, sb_ref = extra
    (o_ref,) = outs

    @pl.when(j < n_nope)
    def _():
        o_ref[...] = (product() * scale).astype(o_ref.dtype)

    @pl.when(j >= n_nope)
    def _():
        acc = product()
        cos = cos_ref[...]
        sa = sa_ref[...]
        sb = sb_ref[...]
        for h in range(acc.shape[1] // LANES):
            x = acc[:, h * LANES:(h + 1) * LANES]
            o_ref[:, h * LANES:(h + 1) * LANES] = (_rope64(x, cos, sa, sb) * scale).astype(o_ref.dtype)


def _mla_q_proj(cq, w, tabs, *, nope_w, scale, seq, out_dtype):
    m = cq.shape[0]
    n = w.shape[1]
    tm = _tile(min(m, seq), 1024)
    tn = _tile(nope_w, 1024)
    nseq = seq // tm
    tab = pl.BlockSpec((tm, LANES), lambda i, j: (i % nseq, 0))
    (out,) = _matmul(
        cq, w, name="mla_q_proj", extras=tuple(tabs), extra_specs=(tab, tab, tab),
        epilogue=functools.partial(_mla_q_epilogue, n_nope=nope_w // tn, scale=scale),
        out_shapes=[jax.ShapeDtypeStruct((m, n), out_dtype)],
        out_specs=[pl.BlockSpec((tm, tn), lambda i, j: (i, j))],
        tm=tm, tn=tn)
    return out


def _cast_epilogue(j, product, extra, outs):
    (o_ref,) = outs
    o_ref[...] = product().astype(o_ref.dtype)


def _plain_matmul(a, w, out_dtype):
    m = a.shape[0]
    n = w.shape[1]
    tm = _tile(m, 1024)
    tn = _tile(n, 1024)
    (out,) = _matmul(a, w, name="mla_k_expand", epilogue=_cast_epilogue,
                     out_shapes=[jax.ShapeDtypeStruct((m, n), out_dtype)],
                     out_specs=[pl.BlockSpec((tm, tn), lambda i, j: (i, j))], tm=tm, tn=tn)
    return out


FLASH_TILE = 512
FLASH_HEADS_PER_STEP = 4


def _vt_kernel(x_ref, wt_ref, o_ref):
    o_ref[0] = _nt_dot(wt_ref[...], x_ref[...]).astype(o_ref.dtype)


def _mla_v_expand_t(ckvb, w_uv_t, tq):
    m, c = ckvb.shape
    n = w_uv_t.shape[0]
    tn = _tile(n, 1024)
    return pl.pallas_call(
        _vt_kernel,
        grid=(m // tq, n // tn),
        in_specs=[pl.BlockSpec((tq, c), lambda i, j: (i, 0)), pl.BlockSpec((tn, c), lambda i, j: (j, 0))],
        out_specs=pl.BlockSpec((1, tn, tq), lambda i, j: (i, j, 0)),
        out_shape=jax.ShapeDtypeStruct((m // tq, n, tq), BF16),
        compiler_params=_params("parallel", "arbitrary"),
        name="mla_v_expand_t",
    )(ckvb, w_uv_t)


def _flash_kernel(qn_ref, qr_ref, kn_ref, kr_ref, vt_ref, o_ref, *, tq, hps):
    qi = pl.program_id(2)
    dv = vt_ref.shape[1] // hps
    qs = [jnp.concatenate([qn_ref[:, h * LANES:(h + 1) * LANES], qr_ref[:, h * LANES:(h + 1) * LANES]], axis=1)
          for h in range(hps)]

    def scores(h, start):
        k = jnp.concatenate([kn_ref[pl.ds(start, tq), h * LANES:(h + 1) * LANES], kr_ref[pl.ds(start, tq), :]],
                            axis=1)
        return _nt_dot(k, qs[h])

    ones = jnp.ones((BF16_SUBLANES, tq), BF16)

    def probs(carry, s):
        m, _ = carry
        m_new = jnp.maximum(m, jnp.max(s, axis=0, keepdims=True))
        return m_new, jnp.exp2(m - m_new), jnp.exp2(s - m_new).astype(BF16)

    def update_all(carries, s_all, kb):
        st = [probs(carries[h], s_all[h]) for h in range(hps)]
        out = []
        for h in range(hps):
            m_new, corr, pb = st[h]
            vt = jnp.concatenate([vt_ref[kb, h * dv:(h + 1) * dv, :], ones], axis=0)
            out.append((m_new, carries[h][1] * corr + jnp.dot(vt, pb, preferred_element_type=F32)))
        return tuple(out)

    def body(kb, carries):
        start = pl.multiple_of(kb * tq, tq)
        return update_all(carries, [scores(h, start) for h in range(hps)], kb)

    init = tuple((jnp.full((1, tq), NEG, F32), jnp.zeros((dv + BF16_SUBLANES, tq), F32)) for _ in range(hps))
    carries = lax.fori_loop(0, qi, body, init)
    start = pl.multiple_of(qi * tq, tq)
    causal = lax.broadcasted_iota(jnp.int32, (tq, tq), 0) <= lax.broadcasted_iota(jnp.int32, (tq, tq), 1)
    final = update_all(carries, [jnp.where(causal, scores(h, start), NEG) for h in range(hps)], qi)
    for h in range(hps):
        acc = final[h][1]
        o_ref[:, h * dv:(h + 1) * dv] = (acc[:dv] / acc[dv:dv + 1]).T.astype(o_ref.dtype)


def _mla_prompt_attention(q, kn, krb, vt, *, batch, seq, heads):
    m = q.shape[0]
    tq = vt.shape[2]
    nq = seq // tq
    hps = FLASH_HEADS_PER_STEP if heads % FLASH_HEADS_PER_STEP == 0 else 1
    w = hps * LANES
    ng = heads // hps
    qblk = lambda off: pl.BlockSpec((tq, w), lambda b, h, i: (b * nq + i, off + h))
    return pl.pallas_call(
        functools.partial(_flash_kernel, tq=tq, hps=hps),
        grid=(batch, ng, nq),
        in_specs=[qblk(0), qblk(ng),
                  pl.BlockSpec((seq, w), lambda b, h, i: (b, h)),
                  pl.BlockSpec((seq, LANES), lambda b, h, i: (b, 0)),
                  pl.BlockSpec((nq, w, tq), lambda b, h, i: (b, h, 0))],
        out_specs=pl.BlockSpec((tq, w), lambda b, h, i: (b * nq + i, h)),
        out_shape=jax.ShapeDtypeStruct((m, heads * LANES), BF16),
        compiler_params=_params("parallel", "parallel", "arbitrary"),
        name="mla_prompt_flash",
    )(q, q, kn, krb, vt)


def _head_nt_kernel(a_ref, w_ref, o_ref):
    o_ref[0] = _nt_dot(a_ref[...], w_ref[...]).astype(o_ref.dtype)


def _absorb_uk(qn, w_uk2, *, heads):
    b = qn.shape[0]
    c = w_uk2.shape[0]
    return pl.pallas_call(
        _head_nt_kernel,
        grid=(heads,),
        in_specs=[pl.BlockSpec((b, LANES), lambda h: (0, h)), pl.BlockSpec((c, LANES), lambda h: (0, h))],
        out_specs=pl.BlockSpec((1, b, c), lambda h: (h, 0, 0)),
        out_shape=jax.ShapeDtypeStruct((heads, b, c), F32),
        compiler_params=_params("parallel"),
        name="mla_absorb_uk",
    )(qn, w_uk2)


def _head_nn_kernel(a_ref, w_ref, o_ref):
    o_ref[...] = jnp.dot(a_ref[0], w_ref[...], preferred_element_type=F32).astype(o_ref.dtype)


def _expand_uv(o_lat, w_uv2, *, heads):
    b = o_lat.shape[1]
    c = o_lat.shape[2]
    return pl.pallas_call(
        _head_nn_kernel,
        grid=(heads,),
        in_specs=[pl.BlockSpec((1, b, c), lambda h: (h, 0, 0)), pl.BlockSpec((c, LANES), lambda h: (0, h))],
        out_specs=pl.BlockSpec((b, LANES), lambda h: (0, h)),
        out_shape=jax.ShapeDtypeStruct((b, heads * LANES), BF16),
        compiler_params=_params("parallel"),
        name="mla_expand_uv",
    )(o_lat, w_uv2)


def _paged_kernel(pt_ref, ql_ref, qr_ref, cn_ref, kn_ref, *rest, sps, pps, rope_dim):
    n = sps * pps
    ckv_refs = rest[:n]
    kr_refs = rest[n:2 * n]
    o_ref, m_ref, l_ref, acc_ref = rest[2 * n:]
    g = pl.program_id(1)

    @pl.when(g == 0)
    def _():
        m_ref[...] = jnp.full_like(m_ref, NEG)
        l_ref[...] = jnp.zeros_like(l_ref)
        acc_ref[...] = jnp.zeros_like(acc_ref)

    ql = [ql_ref[s] for s in range(sps)]
    qr = [qr_ref[s][:, :rope_dim] for s in range(sps)]
    pages = [[ckv_refs[s * pps + i][0, 0].astype(BF16) for i in range(pps)] for s in range(sps)]
    page = pages[0][0].shape[0]
    scores = []
    for s in range(sps):
        qlb = ql[s].astype(BF16)
        qrb = qr[s].astype(BF16)
        scores.append(jnp.concatenate(
            [_nt_dot(qlb, pages[s][i])
             + jnp.dot(qrb, kr_refs[s * pps + i][0, 0].astype(BF16), preferred_element_type=F32)
             for i in range(pps)], axis=1))
    stats = []
    for s in range(sps):
        m = m_ref[s]
        m_new = jnp.maximum(m, jnp.max(scores[s], axis=-1, keepdims=True))
        corr = jnp.exp(m - m_new)
        p = jnp.exp(scores[s] - m_new)
        l_new = l_ref[s] * corr + jnp.sum(p, axis=-1, keepdims=True)
        stats.append((m_new, corr, l_new, p.astype(BF16)))
    accs = []
    for s in range(sps):
        m_new, corr, l_new, pb = stats[s]
        pv = jnp.dot(pb[:, :page], pages[s][0], preferred_element_type=F32)
        for i in range(1, pps):
            pv = pv + jnp.dot(pb[:, i * page:(i + 1) * page], pages[s][i], preferred_element_type=F32)
        acc_new = acc_ref[s] * corr + pv
        m_ref[s] = m_new
        l_ref[s] = l_new
        acc_ref[s] = acc_new
        accs.append(acc_new)

    @pl.when(g == pl.num_programs(1) - 1)
    def _():
        for s in range(sps):
            m_new, _, l_new, _ = stats[s]
            cn = cn_ref[s]
            s_new = (jnp.sum(ql[s] * cn, axis=-1, keepdims=True)
                     + jnp.sum(qr[s] * kn_ref[s][:, :rope_dim], axis=-1, keepdims=True))
            m_fin = jnp.maximum(m_new, s_new)
            c_fin = jnp.exp(m_new - m_fin)
            p_new = jnp.exp(s_new - m_fin)
            l_fin = l_new * c_fin + p_new
            o_ref[s] = (accs[s] * c_fin + p_new * cn) / l_fin


PAGES_PER_STEP = 16
SEQS_PER_STEP = 2


def _mla_paged_attention(q_lat, qr, ckv_new, kr_new, cache_ckv, cache_kr_t, page_table, *, layer):
    b, heads, c = q_lat.shape
    n_pages = page_table.shape[1]
    page = cache_ckv.shape[2]
    rope_dim = cache_kr_t.shape[2]
    pps = _tile(n_pages, PAGES_PER_STEP)
    sps = SEQS_PER_STEP if b % SEQS_PER_STEP == 0 else 1
    per_b = lambda width: pl.BlockSpec((sps, heads, width), lambda i, g, pt: (i, 0, 0))
    new_b = lambda width: pl.BlockSpec((sps, 1, width), lambda i, g, pt: (i, 0, 0))

    def page_spec(rows, width, s, k):
        return pl.BlockSpec((1, 1, rows, width), lambda i, g, pt: (layer, pt[i * sps + s, g * pps + k], 0, 0))

    slots = [(s, k) for s in range(sps) for k in range(pps)]
    grid_spec = pltpu.PrefetchScalarGridSpec(
        num_scalar_prefetch=1,
        grid=(b // sps, n_pages // pps),
        in_specs=[per_b(c), per_b(LANES), new_b(c), new_b(LANES)]
        + [page_spec(page, c, s, k) for s, k in slots] + [page_spec(rope_dim, page, s, k) for s, k in slots],
        out_specs=per_b(c),
        scratch_shapes=[pltpu.VMEM((sps, heads, 1), F32), pltpu.VMEM((sps, heads, 1), F32),
                        pltpu.VMEM((sps, heads, c), F32)],
    )
    return pl.pallas_call(
        functools.partial(_paged_kernel, sps=sps, pps=pps, rope_dim=rope_dim),
        grid_spec=grid_spec,
        out_shape=jax.ShapeDtypeStruct((b, heads, c), F32),
        compiler_params=_params("parallel", "arbitrary"),
        name="mla_paged_attention",
    )(page_table, q_lat, qr, ckv_new.reshape(b, 1, c), kr_new.reshape(b, 1, LANES),
      *([cache_ckv] * len(slots)), *([cache_kr_t] * len(slots)))


def _concat_layers(parts):
    return parts[0] if len(parts) == 1 else jnp.concatenate(parts, axis=0)


def kernel(x_prompt, x_sample, state_ret, cache_kv_latent, cache_k_rope, page_table, norm_mix, norm_mlp, norm_final, w_ret_in, ret_gn_w, ret_gn_b, w_ret_out, w_mla_in, mla_q_norm, mla_kv_norm, w_mla_uq, w_mla_uk, w_mla_uv, w_mla_out, w_up, w_down):
    bp, seq, d = x_prompt.shape
    bs, ls, _ = x_sample.shape
    assert ls == 1, "the sample group is a single new token per sequence"
    depth = norm_mix.shape[0]
    assert depth >= 1, "the final norm is fused into the last layer's MLP"
    ret_heads, ret_dk, ret_dv = state_ret.shape[2:]
    qk_w = ret_heads * ret_dk
    v_w = ret_heads * ret_dv
    kv_lora, mla_heads, nope = w_mla_uk.shape[1:]
    v_dim = w_mla_uv.shape[3]
    q_lora = mla_q_norm.shape[1]
    rope_dim = cache_k_rope.shape[3]
    page = cache_kv_latent.shape[2]
    assert ret_dk == 2 * LANES and nope == LANES and v_dim == LANES and rope_dim * 2 == LANES
    past = page_table.shape[1] * page
    scale = float((nope + rope_dim) ** -0.5)

    pos_p = jnp.arange(seq, dtype=F32)
    pos_s = jnp.broadcast_to(past + jnp.arange(ls, dtype=F32), (bs,))
    ret_tab_p = _rope_cos_sin(pos_p, ret_dk // 2)
    ret_tab_s = _rope_cos_sin(pos_s, ret_dk // 2)
    mla_tab_p = _rope64_tables(pos_p)
    mla_tab_s = _rope64_tables(pos_s)

    xp = x_prompt.reshape(bp * seq, d)
    xs = x_sample.reshape(bs, d)
    ret_p, ret_s, ckv_p, kr_p, ckv_s, kr_s = [], [], [], [], [], []
    for i in range(depth):
        j = i // 2
        if i % 2 == 0:
            w_in = w_ret_in[j].astype(BF16)
            w_out = w_ret_out[j].astype(BF16)
            hq = _ret_in_proj(xp, norm_mix[i], w_in, *ret_tab_p, qk_w=qk_w, v_w=v_w, seq=seq, out_dtype=BF16)
            y, r_p = _ret_prompt(hq, ret_gn_w[j], ret_gn_b[j], batch=bp, seq=seq,
                                 heads=ret_heads, dk=ret_dk, dv=ret_dv)
            xp = _proj_residual(y, w_out, xp)
            hq = _ret_in_proj(xs, norm_mix[i], w_in, *ret_tab_s, qk_w=qk_w, v_w=v_w, seq=bs, out_dtype=F32)
            y, r_s = _ret_sample(hq, state_ret, ret_gn_w[j], ret_gn_b[j], layer=j,
                                 heads=ret_heads, dk=ret_dk, dv=ret_dv)
            xs = _proj_residual(y, w_out, xs)
            ret_p.append(r_p[None])
            ret_s.append(r_s)
        else:
            w_in = jnp.pad(w_mla_in[j], ((0, 0), (0, LANES - rope_dim))).astype(BF16)
            w_uq = w_mla_uq[j].reshape(q_lora, mla_heads, nope + rope_dim)
            w_uq = jnp.concatenate(
                [w_uq[:, :, :nope].reshape(q_lora, mla_heads * nope),
                 jnp.pad(w_uq[:, :, nope:], ((0, 0), (0, 0), (0, LANES - rope_dim))).reshape(q_lora, mla_heads * LANES)],
                axis=1).astype(BF16)
            w_uk2 = w_mla_uk[j].reshape(kv_lora, mla_heads * nope).astype(BF16)
            w_uv2 = w_mla_uv[j].reshape(kv_lora, mla_heads * v_dim).astype(BF16)
            w_out = w_mla_out[j].astype(BF16)
            nope_w = mla_heads * nope

            cq, ckv, ckvb, kr, krb = _mla_in_proj(xp, norm_mix[i], w_in, mla_q_norm[j], mla_kv_norm[j], mla_tab_p,
                                                  q_lora=q_lora, kv_lora=kv_lora, seq=seq)
            q = _mla_q_proj(cq, w_uq, mla_tab_p, nope_w=nope_w, scale=scale * LOG2_E, seq=seq, out_dtype=BF16)
            kn = _plain_matmul(ckvb, w_uk2, BF16)
            vt = _mla_v_expand_t(ckvb, w_uv2.T, _tile(seq, FLASH_TILE))
            o = _mla_prompt_attention(q, kn, krb, vt, batch=bp, seq=seq, heads=mla_heads)
            xp = _proj_residual(o, w_out, xp)
            ckv_p.append(ckv.reshape(1, bp, seq // page, page, kv_lora))
            kr_p.append(kr[:, :rope_dim].reshape(1, bp, seq // page, page, rope_dim))

            cq, ckv, ckvb, kr, krb = _mla_in_proj(xs, norm_mix[i], w_in, mla_q_norm[j], mla_kv_norm[j], mla_tab_s,
                                                  q_lora=q_lora, kv_lora=kv_lora, seq=bs)
            q = _mla_q_proj(cq, w_uq, mla_tab_s, nope_w=nope_w, scale=scale, seq=bs, out_dtype=F32)
            q_lat = _absorb_uk(q[:, :nope_w].astype(BF16), w_uk2, heads=mla_heads)
            o_lat = _mla_paged_attention(
                q_lat.transpose(1, 0, 2), q[:, nope_w:].reshape(bs, mla_heads, LANES), ckv, kr,
                cache_kv_latent, jnp.swapaxes(cache_k_rope, 2, 3), page_table, layer=j)
            o = _expand_uv(o_lat.transpose(1, 0, 2).astype(BF16), w_uv2, heads=mla_heads)
            xs = _proj_residual(o, w_out, xs)
            ckv_s.append(ckv.reshape(1, bs, ls, kv_lora))
            kr_s.append(kr[:, :rope_dim].reshape(1, bs, ls, rope_dim))
        w_u = w_up[i].astype(BF16)
        w_d = w_down[i].astype(BF16)
        final_gain = norm_final if i == depth - 1 else None
        xp = _mlp_residual(xp, norm_mlp[i], w_u, w_d, final_gain)
        xs = _mlp_residual(xs, norm_mlp[i], w_u, w_d, final_gain)
    y_prompt = xp.reshape(bp, seq, d)
    y_sample = xs.reshape(bs, ls, d)
    return (y_prompt, y_sample) + tuple(_concat_layers(t) for t in (ret_p, ret_s, ckv_p, kr_p, ckv_s, kr_s))
```

```python
import functools

import jax
import jax.numpy as jnp
from jax import lax
from jax.experimental import pallas as pl
from jax.experimental.pallas import tpu as pltpu

ROPE_BASE = 10000.0
NORM_EPS = 1e-6
GN_EPS = 1e-5
NEG = -1e30
RET_CHUNK = 256
LOG2_E = 1.4426950408889634

LANES = 128
BF16_SUBLANES = 16
V7X_VMEM_BYTES = 64 * 1024 * 1024
VMEM_LIMIT = 52 * 1024 * 1024

F32 = jnp.float32
BF16 = jnp.bfloat16


def _params(*sem):
    return pltpu.CompilerParams(dimension_semantics=sem, vmem_limit_bytes=VMEM_LIMIT)


def _tile(n, pref):
    if n <= pref:
        return n
    t = pref
    while n % t:
        t //= 2
    return t


def _nt_dot(a, b):
    return lax.dot_general(a, b, (((1,), (1,)), ((), ())), preferred_element_type=F32)


def _tn_dot(a, b):
    return lax.dot_general(a, b, (((0,), (0,)), ((), ())), preferred_element_type=F32)


def _rope_cos_sin(pos, d2):
    inv = ROPE_BASE ** (-jnp.arange(d2, dtype=F32) / d2)
    ang = pos[:, None] * inv[None, :]
    return jnp.cos(ang), jnp.sin(ang)


def _rope64_tables(pos):
    cos, sin = _rope_cos_sin(pos, 32)
    z = jnp.zeros_like(sin)
    cos_t = jnp.concatenate([cos, cos, cos, cos], axis=-1)
    sa = jnp.concatenate([-sin, z, -sin, z], axis=-1)
    sb = jnp.concatenate([z, sin, z, sin], axis=-1)
    return cos_t, sa, sb


def _rope64(x, cos_t, sa, sb):
    return x * cos_t + pltpu.roll(x, 96, 1) * sa + pltpu.roll(x, 32, 1) * sb


def _mm_kernel(*refs, has_norm, n_extra, epilogue):
    if has_norm:
        x_ref, g_ref, w_ref = refs[:3]
        rest = refs[3:]
        xn_ref = rest[-1]
        rest = rest[:-1]

        @pl.when(pl.program_id(1) == 0)
        def _():
            x = x_ref[...]
            ms = jnp.mean(x * x, axis=-1, keepdims=True)
            xn_ref[...] = (x * lax.rsqrt(ms + NORM_EPS) * g_ref[...]).astype(xn_ref.dtype)

        lhs = xn_ref[...]
    else:
        x_ref, w_ref = refs[:2]
        rest = refs[2:]
        lhs = x_ref[...]
    extra = rest[:n_extra]
    outs = rest[n_extra:]
    product = lambda: jnp.dot(lhs, w_ref[...], preferred_element_type=F32)
    epilogue(pl.program_id(1), product, extra, outs)


def _matmul(x, w, *, name, gain=None, extras=(), extra_specs=(), epilogue, out_shapes, out_specs, tm, tn):
    m, k = x.shape
    n = w.shape[1]
    has_norm = gain is not None
    in_specs = [pl.BlockSpec((tm, k), lambda i, j: (i, 0))]
    args = [x]
    if has_norm:
        in_specs.append(pl.BlockSpec((1, k), lambda i, j: (0, 0)))
        args.append(gain.reshape(1, k).astype(F32))
    in_specs.append(pl.BlockSpec((k, tn), lambda i, j: (0, j)))
    args.append(w)
    in_specs += list(extra_specs)
    args += list(extras)
    scratch = [pltpu.VMEM((tm, k), BF16)] if has_norm else []
    return pl.pallas_call(
        functools.partial(_mm_kernel, has_norm=has_norm, n_extra=len(extras), epilogue=epilogue),
        grid=(m // tm, n // tn),
        in_specs=in_specs,
        out_specs=out_specs,
        out_shape=out_shapes,
        scratch_shapes=scratch,
        compiler_params=_params("parallel", "arbitrary"),
        name=name,
    )(*args)


def _ret_in_epilogue(j, product, extra, outs, *, tn, qk_w, v_w):
    cos_ref, sin_ref = extra
    (o_ref,) = outs
    n_qk = (2 * qk_w) // tn
    n_v = v_w // tn

    @pl.when(j < n_qk)
    def _():
        acc = product()
        cos = cos_ref[...]
        sin = sin_ref[...]
        scale = jnp.where(j >= qk_w // tn, 0.0625, 1.0).astype(F32)
        for h in range(tn // 256):
            x1 = acc[:, h * 256:h * 256 + 128]
            x2 = acc[:, h * 256 + 128:(h + 1) * 256]
            o_ref[:, h * 256:h * 256 + 128] = ((x1 * cos - x2 * sin) * scale).astype(o_ref.dtype)
            o_ref[:, h * 256 + 128:(h + 1) * 256] = ((x2 * cos + x1 * sin) * scale).astype(o_ref.dtype)

    @pl.when(jnp.logical_and(j >= n_qk, j < n_qk + n_v))
    def _():
        o_ref[...] = product().astype(o_ref.dtype)

    @pl.when(j >= n_qk + n_v)
    def _():
        acc = product()
        o_ref[...] = (acc * jax.nn.sigmoid(acc)).astype(o_ref.dtype)


def _ret_in_proj(x, gain, w, cos, sin, *, qk_w, v_w, seq, out_dtype):
    m = x.shape[0]
    n = w.shape[1]
    tm = _tile(min(m, seq), 1024)
    tn = _tile(qk_w, 1024)
    nseq = seq // tm
    tab = pl.BlockSpec((tm, LANES), lambda i, j: (i % nseq, 0))
    (out,) = _matmul(
        x, w, name="ret_in_proj", gain=gain, extras=(cos, sin), extra_specs=(tab, tab),
        epilogue=functools.partial(_ret_in_epilogue, tn=tn, qk_w=qk_w, v_w=v_w),
        out_shapes=[jax.ShapeDtypeStruct((m, n), out_dtype)],
        out_specs=[pl.BlockSpec((tm, tn), lambda i, j: (i, j))],
        tm=tm, tn=tn)
    return out


def _ret_log_decay(heads):
    return jnp.log1p(-jnp.exp2(-5.0 - jnp.arange(heads, dtype=F32)))


def _group_norm_gate(o, gw, gb, gate):
    mu = jnp.mean(o, axis=-1, keepdims=True)
    d = o - mu
    var = jnp.mean(d * d, axis=-1, keepdims=True)
    return (d * lax.rsqrt(var + GN_EPS) * gw + gb) * gate


def _ret_chunk_kernel(h_ref, dec_ref, rs_ref, we_ref, gc_ref, gw_ref, gb_ref, y_ref, st_ref, r_ref,
                      *, heads, dk, dv):
    c = pl.program_id(1)
    qk_w = heads * dk
    v_w = heads * dv

    @pl.when(c == 0)
    def _():
        r_ref[...] = jnp.zeros_like(r_ref)

    for h in range(heads):
        q = h_ref[:, h * dk:(h + 1) * dk]
        k = h_ref[:, qk_w + h * dk:qk_w + (h + 1) * dk]
        v = h_ref[:, 2 * qk_w + h * dv:2 * qk_w + (h + 1) * dv]
        g = h_ref[:, 2 * qk_w + v_w + h * dv:2 * qk_w + v_w + (h + 1) * dv]
        r = r_ref[h]
        s = (_nt_dot(q, k) * dec_ref[h]).astype(BF16)
        rs = jnp.concatenate([rs_ref[h]] * (dv // LANES), axis=1)
        o = jnp.dot(s, v, preferred_element_type=F32)
        o = o + jnp.dot(q, r.astype(BF16), preferred_element_type=F32) * rs
        we = jnp.concatenate([we_ref[h]] * (dk // LANES), axis=1)
        kw = (k.astype(F32) * we).astype(BF16)
        gc = jnp.concatenate([gc_ref[h, 0:1, :]] * (dv // LANES), axis=1)
        r_ref[h] = r * gc + _tn_dot(kw, v)
        gw = gw_ref[:, h * dv:(h + 1) * dv]
        gb = gb_ref[:, h * dv:(h + 1) * dv]
        y_ref[:, h * dv:(h + 1) * dv] = _group_norm_gate(o, gw, gb, g.astype(F32)).astype(y_ref.dtype)

    @pl.when(c == pl.num_programs(1) - 1)
    def _():
        st_ref[0] = r_ref[...]


def _ret_prompt(hq, gn_w, gn_b, *, batch, seq, heads, dk, dv):
    chunk = RET_CHUNK
    nc = seq // chunk
    qk_w = heads * dk
    v_w = heads * dv
    log_g = _ret_log_decay(heads)
    n = jnp.arange(chunk, dtype=F32)
    diff = n[:, None] - n[None, :]
    causal = diff >= 0
    decay = jnp.where(causal[None], jnp.exp(jnp.where(causal, diff, 0.0)[None] * log_g[:, None, None]), 0.0)
    ones = jnp.ones((1, 1, LANES), F32)
    rowscale = jnp.exp((n + 1.0)[None, :] * log_g[:, None])[:, :, None] * ones
    w_end = jnp.exp((chunk - 1.0 - n)[None, :] * log_g[:, None])[:, :, None] * ones
    g_chunk = jnp.exp(chunk * log_g)[:, None, None] * jnp.ones((1, 8, LANES), F32)

    tab = lambda shape: pl.BlockSpec(shape, lambda b, c: (0, 0, 0))
    vec = pl.BlockSpec((1, v_w), lambda b, c: (0, 0))
    y, state = pl.pallas_call(
        functools.partial(_ret_chunk_kernel, heads=heads, dk=dk, dv=dv),
        grid=(batch, nc),
        in_specs=[
            pl.BlockSpec((chunk, hq.shape[1]), lambda b, c: (b * nc + c, 0)),
            tab((heads, chunk, chunk)), tab((heads, chunk, LANES)), tab((heads, chunk, LANES)),
            tab((heads, 8, LANES)), vec, vec,
        ],
        out_specs=[
            pl.BlockSpec((chunk, v_w), lambda b, c: (b * nc + c, 0)),
            pl.BlockSpec((1, heads, dk, dv), lambda b, c: (b, 0, 0, 0)),
        ],
        out_shape=[
            jax.ShapeDtypeStruct((batch * seq, v_w), BF16),
            jax.ShapeDtypeStruct((batch, heads, dk, dv), F32),
        ],
        scratch_shapes=[pltpu.VMEM((heads, dk, dv), F32)],
        compiler_params=_params("parallel", "arbitrary"),
        name="ret_chunk_scan",
    )(hq, decay, rowscale, w_end, g_chunk,
      gn_w.reshape(1, v_w).astype(F32), gn_b.reshape(1, v_w).astype(F32))
    return y, state


def _column(row):
    n = row.shape[1]
    eye = lax.broadcasted_iota(jnp.int32, (n, n), 0) == lax.broadcasted_iota(jnp.int32, (n, n), 1)
    return jnp.sum(jnp.where(eye, row, 0.0), axis=1, keepdims=True)


def _ret_sample_kernel(gam_ref, h_ref, st_ref, gw_ref, gb_ref, y_ref, ns_ref, *, heads, dk, dv):
    qk_w = heads * dk
    v_w = heads * dv
    for h in range(heads):
        gam = gam_ref[h]
        q = h_ref[0, :, h * dk:(h + 1) * dk]
        k = h_ref[0, :, qk_w + h * dk:qk_w + (h + 1) * dk]
        v = h_ref[0, :, 2 * qk_w + h * dv:2 * qk_w + (h + 1) * dv]
        g = h_ref[0, :, 2 * qk_w + v_w + h * dv:2 * qk_w + v_w + (h + 1) * dv]
        r = st_ref[0, 0, h]
        s = jnp.sum(q * k, axis=-1, keepdims=True)
        o = s * v + jnp.sum(r * _column(q), axis=0, keepdims=True) * gam
        ns_ref[0, 0, h] = r * gam + _column(k) * v
        gw = gw_ref[:, h * dv:(h + 1) * dv]
        gb = gb_ref[:, h * dv:(h + 1) * dv]
        y_ref[0, :, h * dv:(h + 1) * dv] = _group_norm_gate(o, gw, gb, g).astype(y_ref.dtype)


def _ret_sample(hq, state, gn_w, gn_b, *, layer, heads, dk, dv):
    batch = hq.shape[0]
    n = hq.shape[1]
    v_w = heads * dv
    gam = jnp.exp(1.0 * _ret_log_decay(heads))
    y, new_state = pl.pallas_call(
        functools.partial(_ret_sample_kernel, heads=heads, dk=dk, dv=dv),
        grid=(batch,),
        in_specs=[
            pl.BlockSpec(memory_space=pltpu.SMEM),
            pl.BlockSpec((1, 1, n), lambda b: (b, 0, 0)),
            pl.BlockSpec((1, 1, heads, dk, dv), lambda b: (layer, b, 0, 0, 0)),
            pl.BlockSpec((1, v_w), lambda b: (0, 0)),
            pl.BlockSpec((1, v_w), lambda b: (0, 0)),
        ],
        out_specs=[
            pl.BlockSpec((1, 1, v_w), lambda b: (b, 0, 0)),
            pl.BlockSpec((1, 1, heads, dk, dv), lambda b: (0, b, 0, 0, 0)),
        ],
        out_shape=[
            jax.ShapeDtypeStruct((batch, 1, v_w), BF16),
            jax.ShapeDtypeStruct((1,) + state.shape[1:], F32),
        ],
        compiler_params=_params("parallel"),
        name="ret_sample_step",
    )(gam, hq.reshape(batch, 1, n), state,
      gn_w.reshape(1, v_w).astype(F32), gn_b.reshape(1, v_w).astype(F32))
    return y.reshape(batch, v_w), new_state


def _residual_epilogue(j, product, extra, outs):
    (res_ref,) = extra
    (o_ref,) = outs
    o_ref[...] = res_ref[...] + product()


def _proj_residual(a, w, res):
    m = a.shape[0]
    n = w.shape[1]
    tm = _tile(m, 512)
    tn = _tile(n, 1024)
    blk = pl.BlockSpec((tm, tn), lambda i, j: (i, j))
    (out,) = _matmul(a, w, name="proj_residual", extras=(res,), extra_specs=(blk,), epilogue=_residual_epilogue,
                     out_shapes=[jax.ShapeDtypeStruct((m, n), F32)], out_specs=[blk], tm=tm, tn=tn)
    return out


def _mlp_kernel(x_ref, g_ref, wu_ref, wd_ref, *rest, final_norm):
    if final_norm:
        fg_ref, o_ref, xn_ref = rest
    else:
        o_ref, xn_ref = rest
    f = pl.program_id(1)

    @pl.when(f == 0)
    def _():
        x = x_ref[...]
        ms = jnp.mean(x * x, axis=-1, keepdims=True)
        xn_ref[...] = (x * lax.rsqrt(ms + NORM_EPS) * g_ref[...]).astype(xn_ref.dtype)
        o_ref[...] = x

    h = jnp.dot(xn_ref[...], wu_ref[...], preferred_element_type=F32)
    h = jnp.square(jnp.maximum(h, 0.0)).astype(BF16)
    o_ref[...] += jnp.dot(h, wd_ref[...], preferred_element_type=F32)

    if final_norm:
        @pl.when(f == pl.num_programs(1) - 1)
        def _():
            y = o_ref[...]
            ms = jnp.mean(y * y, axis=-1, keepdims=True)
            o_ref[...] = y * lax.rsqrt(ms + NORM_EPS) * fg_ref[...]


def _mlp_residual(x, gain, w_up, w_down, final_gain=None):
    m, d = x.shape
    ff = w_up.shape[1]
    tm = _tile(m, 512)
    tf = _tile(ff, 1024)
    final_norm = final_gain is not None
    vec = pl.BlockSpec((1, d), lambda i, f: (0, 0))
    in_specs = [
        pl.BlockSpec((tm, d), lambda i, f: (i, 0)),
        vec,
        pl.BlockSpec((d, tf), lambda i, f: (0, f)),
        pl.BlockSpec((tf, d), lambda i, f: (f, 0)),
    ]
    args = [x, gain.reshape(1, d).astype(F32), w_up, w_down]
    if final_norm:
        in_specs.append(vec)
        args.append(final_gain.reshape(1, d).astype(F32))
    return pl.pallas_call(
        functools.partial(_mlp_kernel, final_norm=final_norm),
        grid=(m // tm, ff // tf),
        in_specs=in_specs,
        out_specs=pl.BlockSpec((tm, d), lambda i, f: (i, 0)),
        out_shape=jax.ShapeDtypeStruct((m, d), F32),
        scratch_shapes=[pltpu.VMEM((tm, d), BF16)],
        compiler_params=_params("parallel", "arbitrary"),
        name="mlp_residual",
    )(*args)


def _mla_in_epilogue(j, product, extra, outs, *, q_lora, kv_lora):
    qg_ref, kg_ref, cos_ref, sa_ref, sb_ref = extra
    cq_ref, ckv_ref, ckvb_ref, kr_ref, krb_ref = outs
    acc = product()

    def norm(x, g):
        return x * lax.rsqrt(jnp.mean(x * x, axis=-1, keepdims=True) + NORM_EPS) * g

    cq_ref[...] = norm(acc[:, :q_lora], qg_ref[...]).astype(cq_ref.dtype)
    ckv = norm(acc[:, q_lora:q_lora + kv_lora], kg_ref[...])
    ckv_ref[...] = ckv
    ckvb_ref[...] = ckv.astype(ckvb_ref.dtype)
    kr = _rope64(acc[:, q_lora + kv_lora:], cos_ref[...], sa_ref[...], sb_ref[...])
    kr_ref[...] = kr
    krb_ref[...] = kr.astype(krb_ref.dtype)


def _mla_in_proj(x, gain, w, q_gain, kv_gain, tabs, *, q_lora, kv_lora, seq):
    m = x.shape[0]
    n = w.shape[1]
    tm = _tile(min(m, seq), 512)
    nseq = seq // tm
    tab = pl.BlockSpec((tm, LANES), lambda i, j: (i % nseq, 0))
    vec = lambda width: pl.BlockSpec((1, width), lambda i, j: (0, 0))
    blk = lambda width: pl.BlockSpec((tm, width), lambda i, j: (i, 0))
    return _matmul(
        x, w, name="mla_in_proj", gain=gain,
        extras=(q_gain.reshape(1, q_lora).astype(F32), kv_gain.reshape(1, kv_lora).astype(F32)) + tuple(tabs),
        extra_specs=(vec(q_lora), vec(kv_lora), tab, tab, tab),
        epilogue=functools.partial(_mla_in_epilogue, q_lora=q_lora, kv_lora=kv_lora),
        out_shapes=[jax.ShapeDtypeStruct((m, q_lora), BF16),
                    jax.ShapeDtypeStruct((m, kv_lora), F32),
                    jax.ShapeDtypeStruct((m, kv_lora), BF16),
                    jax.ShapeDtypeStruct((m, LANES), F32),
                    jax.ShapeDtypeStruct((m, LANES), BF16)],
        out_specs=[blk(q_lora), blk(kv_lora), blk(kv_lora), blk(LANES), blk(LANES)],
        tm=tm, tn=n)


def _mla_q_epilogue(j, product, extra, outs, *, n_nope, scale):
    cos_ref, sa_ref, sb_ref = extra
    (o_ref,) = outs

    @pl.when(j < n_nope)
    def _():
        o_ref[...] = (product() * scale).astype(o_ref.dtype)

    @pl.when(j >= n_nope)
    def _():
        acc = product()
        cos = cos_ref[...]
        sa = sa_ref[...]
        sb = sb_ref[...]
        for h in range(acc.shape[1] // LANES):
            x = acc[:, h * LANES:(h + 1) * LANES]
            o_ref[:, h * LANES:(h + 1) * LANES] = (_rope64(x, cos, sa, sb) * scale).astype(o_ref.dtype)


def _mla_q_proj(cq, w, tabs, *, nope_w, scale, seq, out_dtype):
    m = cq.shape[0]
    n = w.shape[1]
    tm = _tile(min(m, seq), 1024)
    tn = _tile(nope_w, 1024)
    nseq = seq // tm
    tab = pl.BlockSpec((tm, LANES), lambda i, j: (i % nseq, 0))
    (out,) = _matmul(
        cq, w, name="mla_q_proj", extras=tuple(tabs), extra_specs=(tab, tab, tab),
        epilogue=functools.partial(_mla_q_epilogue, n_nope=nope_w // tn, scale=scale),
        out_shapes=[jax.ShapeDtypeStruct((m, n), out_dtype)],
        out_specs=[pl.BlockSpec((tm, tn), lambda i, j: (i, j))],
        tm=tm, tn=tn)
    return out


def _cast_epilogue(j, product, extra, outs):
    (o_ref,) = outs
    o_ref[...] = product().astype(o_ref.dtype)


def _plain_matmul(a, w, out_dtype):
    m = a.shape[0]
    n = w.shape[1]
    tm = _tile(m, 1024)
    tn = _tile(n, 1024)
    (out,) = _matmul(a, w, name="mla_k_expand", epilogue=_cast_epilogue,
                     out_shapes=[jax.ShapeDtypeStruct((m, n), out_dtype)],
                     out_specs=[pl.BlockSpec((tm, tn), lambda i, j: (i, j))], tm=tm, tn=tn)
    return out


FLASH_TILE = 512
FLASH_HEADS_PER_STEP = 4


def _vt_kernel(x_ref, wt_ref, o_ref):
    o_ref[0] = _nt_dot(wt_ref[...], x_ref[...]).astype(o_ref.dtype)


def _mla_v_expand_t(ckvb, w_uv_t, tq):
    m, c = ckvb.shape
    n = w_uv_t.shape[0]
    tn = _tile(n, 1024)
    return pl.pallas_call(
        _vt_kernel,
        grid=(m // tq, n // tn),
        in_specs=[pl.BlockSpec((tq, c), lambda i, j: (i, 0)), pl.BlockSpec((tn, c), lambda i, j: (j, 0))],
        out_specs=pl.BlockSpec((1, tn, tq), lambda i, j: (i, j, 0)),
        out_shape=jax.ShapeDtypeStruct((m // tq, n, tq), BF16),
        compiler_params=_params("parallel", "arbitrary"),
        name="mla_v_expand_t",
    )(ckvb, w_uv_t)


def _flash_kernel(qn_ref, qr_ref, kn_ref, kr_ref, vt_ref, o_ref, *, tq, hps):
    qi = pl.program_id(2)
    dv = vt_ref.shape[1] // hps
    qs = [jnp.concatenate([qn_ref[:, h * LANES:(h + 1) * LANES], qr_ref[:, h * LANES:(h + 1) * LANES]], axis=1)
          for h in range(hps)]

    def scores(h, start):
        k = jnp.concatenate([kn_ref[pl.ds(start, tq), h * LANES:(h + 1) * LANES], kr_ref[pl.ds(start, tq), :]],
                            axis=1)
        return _nt_dot(k, qs[h])

    ones = jnp.ones((BF16_SUBLANES, tq), BF16)

    def probs(carry, s):
        m, _ = carry
        m_new = jnp.maximum(m, jnp.max(s, axis=0, keepdims=True))
        return m_new, jnp.exp2(m - m_new), jnp.exp2(s - m_new).astype(BF16)

    def update_all(carries, s_all, kb):
        st = [probs(carries[h], s_all[h]) for h in range(hps)]
        out = []
        for h in range(hps):
            m_new, corr, pb = st[h]
            vt = jnp.concatenate([vt_ref[kb, h * dv:(h + 1) * dv, :], ones], axis=0)
            out.append((m_new, carries[h][1] * corr + jnp.dot(vt, pb, preferred_element_type=F32)))
        return tuple(out)

    def body(kb, carries):
        start = pl.multiple_of(kb * tq, tq)
        return update_all(carries, [scores(h, start) for h in range(hps)], kb)

    init = tuple((jnp.full((1, tq), NEG, F32), jnp.zeros((dv + BF16_SUBLANES, tq), F32)) for _ in range(hps))
    carries = lax.fori_loop(0, qi, body, init)
    start = pl.multiple_of(qi * tq, tq)
    causal = lax.broadcasted_iota(jnp.int32, (tq, tq), 0) <= lax.broadcasted_iota(jnp.int32, (tq, tq), 1)
    final = update_all(carries, [jnp.where(causal, scores(h, start), NEG) for h in range(hps)], qi)
    for h in range(hps):
        acc = final[h][1]
        o_ref[:, h * dv:(h + 1) * dv] = (acc[:dv] / acc[dv:dv + 1]).T.astype(o_ref.dtype)


def _mla_prompt_attention(q, kn, krb, vt, *, batch, seq, heads):
    m = q.shape[0]
    tq = vt.shape[2]
    nq = seq // tq
    hps = FLASH_HEADS_PER_STEP if heads % FLASH_HEADS_PER_STEP == 0 else 1
    w = hps * LANES
    ng = heads // hps
    qblk = lambda off: pl.BlockSpec((tq, w), lambda b, h, i: (b * nq + i, off + h))
    return pl.pallas_call(
        functools.partial(_flash_kernel, tq=tq, hps=hps),
        grid=(batch, ng, nq),
        in_specs=[qblk(0), qblk(ng),
                  pl.BlockSpec((seq, w), lambda b, h, i: (b, h)),
                  pl.BlockSpec((seq, LANES), lambda b, h, i: (b, 0)),
                  pl.BlockSpec((nq, w, tq), lambda b, h, i: (b, h, 0))],
        out_specs=pl.BlockSpec((tq, w), lambda b, h, i: (b * nq + i, h)),
        out_shape=jax.ShapeDtypeStruct((m, heads * LANES), BF16),
        compiler_params=_params("parallel", "parallel", "arbitrary"),
        name="mla_prompt_flash",
    )(q, q, kn, krb, vt)


def _head_nt_kernel(a_ref, w_ref, o_ref):
    o_ref[0] = _nt_dot(a_ref[...], w_ref[...]).astype(o_ref.dtype)


def _absorb_uk(qn, w_uk2, *, heads):
    b = qn.shape[0]
    c = w_uk2.shape[0]
    return pl.pallas_call(
        _head_nt_kernel,
        grid=(heads,),
        in_specs=[pl.BlockSpec((b, LANES), lambda h: (0, h)), pl.BlockSpec((c, LANES), lambda h: (0, h))],
        out_specs=pl.BlockSpec((1, b, c), lambda h: (h, 0, 0)),
        out_shape=jax.ShapeDtypeStruct((heads, b, c), F32),
        compiler_params=_params("parallel"),
        name="mla_absorb_uk",
    )(qn, w_uk2)


def _head_nn_kernel(a_ref, w_ref, o_ref):
    o_ref[...] = jnp.dot(a_ref[0], w_ref[...], preferred_element_type=F32).astype(o_ref.dtype)


def _expand_uv(o_lat, w_uv2, *, heads):
    b = o_lat.shape[1]
    c = o_lat.shape[2]
    return pl.pallas_call(
        _head_nn_kernel,
        grid=(heads,),
        in_specs=[pl.BlockSpec((1, b, c), lambda h: (h, 0, 0)), pl.BlockSpec((c, LANES), lambda h: (0, h))],
        out_specs=pl.BlockSpec((b, LANES), lambda h: (0, h)),
        out_shape=jax.ShapeDtypeStruct((b, heads * LANES), BF16),
        compiler_params=_params("parallel"),
        name="mla_expand_uv",
    )(o_lat, w_uv2)


def _paged_kernel(pt_ref, ql_ref, qr_ref, cn_ref, kn_ref, ckv_hbm, kr_hbm, o_ref,
                  ckv_buf, kr_buf, sem, m_ref, l_ref, acc_ref, *, layer, sps, pps, rope_dim):
    n_groups = pl.num_programs(1)
    g = pl.program_id(1)
    step = pl.program_id(0) * n_groups + g
    n_steps = pl.num_programs(0) * n_groups
    slot = lax.rem(step, 2)

    def gather(t, buf_slot):
        row0 = (t // n_groups) * sps
        col0 = lax.rem(t, n_groups) * pps
        out = []
        for s in range(sps):
            for k in range(pps):
                pid = pt_ref[row0 + s, col0 + k]
                j = s * pps + k
                out.append(pltpu.make_async_copy(ckv_hbm.at[layer, pid], ckv_buf.at[buf_slot, j], sem.at[0, buf_slot]))
                out.append(pltpu.make_async_copy(kr_hbm.at[layer, pid], kr_buf.at[buf_slot, j], sem.at[1, buf_slot]))
        return out

    @pl.when(step == 0)
    def _():
        for c in gather(step, slot):
            c.start()

    @pl.when(step + 1 < n_steps)
    def _():
        for c in gather(step + 1, 1 - slot):
            c.start()

    for c in gather(step, slot):
        c.wait()

    @pl.when(g == 0)
    def _():
        m_ref[...] = jnp.full_like(m_ref, NEG)
        l_ref[...] = jnp.zeros_like(l_ref)
        acc_ref[...] = jnp.zeros_like(acc_ref)

    ql = [ql_ref[s] for s in range(sps)]
    qr = [qr_ref[s][:, :rope_dim] for s in range(sps)]
    pages = [[ckv_buf[slot, s * pps + i].astype(BF16) for i in range(pps)] for s in range(sps)]
    page = pages[0][0].shape[0]
    scores = []
    for s in range(sps):
        qlb = ql[s].astype(BF16)
        qrb = qr[s].astype(BF16)
        scores.append(jnp.concatenate(
            [_nt_dot(qlb, pages[s][i])
             + jnp.dot(qrb, kr_buf[slot, s * pps + i].astype(BF16), preferred_element_type=F32)
             for i in range(pps)], axis=1))
    stats = []
    for s in range(sps):
        m = m_ref[s]
        m_new = jnp.maximum(m, jnp.max(scores[s], axis=-1, keepdims=True))
        corr = jnp.exp(m - m_new)
        p = jnp.exp(scores[s] - m_new)
        l_new = l_ref[s] * corr + jnp.sum(p, axis=-1, keepdims=True)
        stats.append((m_new, corr, l_new, p.astype(BF16)))
    accs = []
    for s in range(sps):
        m_new, corr, l_new, pb = stats[s]
        pv = jnp.dot(pb[:, :page], pages[s][0], preferred_element_type=F32)
        for i in range(1, pps):
            pv = pv + jnp.dot(pb[:, i * page:(i + 1) * page], pages[s][i], preferred_element_type=F32)
        acc_new = acc_ref[s] * corr + pv
        m_ref[s] = m_new
        l_ref[s] = l_new
        acc_ref[s] = acc_new
        accs.append(acc_new)

    @pl.when(g == pl.num_programs(1) - 1)
    def _():
        for s in range(sps):
            m_new, _, l_new, _ = stats[s]
            cn = cn_ref[s]
            s_new = (jnp.sum(ql[s] * cn, axis=-1, keepdims=True)
                     + jnp.sum(qr[s] * kn_ref[s][:, :rope_dim], axis=-1, keepdims=True))
            m_fin = jnp.maximum(m_new, s_new)
            c_fin = jnp.exp(m_new - m_fin)
            p_new = jnp.exp(s_new - m_fin)
            l_fin = l_new * c_fin + p_new
            o_ref[s] = (accs[s] * c_fin + p_new * cn) / l_fin


PAGES_PER_STEP = 16
SEQS_PER_STEP = 2


def _mla_paged_attention(q_lat, qr, ckv_new, kr_new, cache_ckv, cache_kr_t, page_table, *, layer):
    b, heads, c = q_lat.shape
    n_pages = page_table.shape[1]
    page = cache_ckv.shape[2]
    rope_dim = cache_kr_t.shape[2]
    pps = _tile(n_pages, PAGES_PER_STEP)
    sps = SEQS_PER_STEP if b % SEQS_PER_STEP == 0 else 1
    per_b = lambda width: pl.BlockSpec((sps, heads, width), lambda i, g, pt: (i, 0, 0))
    new_b = lambda width: pl.BlockSpec((sps, 1, width), lambda i, g, pt: (i, 0, 0))

    hbm = pl.BlockSpec(memory_space=pl.ANY)
    n_slots = sps * pps
    grid_spec = pltpu.PrefetchScalarGridSpec(
        num_scalar_prefetch=1,
        grid=(b // sps, n_pages // pps),
        in_specs=[per_b(c), per_b(LANES), new_b(c), new_b(LANES), hbm, hbm],
        out_specs=per_b(c),
        scratch_shapes=[pltpu.VMEM((2, n_slots, page, c), cache_ckv.dtype),
                        pltpu.VMEM((2, n_slots, rope_dim, page), cache_kr_t.dtype),
                        pltpu.SemaphoreType.DMA((2, 2)),
                        pltpu.VMEM((sps, heads, 1), F32), pltpu.VMEM((sps, heads, 1), F32),
                        pltpu.VMEM((sps, heads, c), F32)],
    )
    return pl.pallas_call(
        functools.partial(_paged_kernel, layer=layer, sps=sps, pps=pps, rope_dim=rope_dim),
        grid_spec=grid_spec,
        out_shape=jax.ShapeDtypeStruct((b, heads, c), F32),
        compiler_params=_params("arbitrary", "arbitrary"),
        name="mla_paged_attention",
    )(page_table, q_lat, qr, ckv_new.reshape(b, 1, c), kr_new.reshape(b, 1, LANES), cache_ckv, cache_kr_t)


def _concat_layers(parts):
    return parts[0] if len(parts) == 1 else jnp.concatenate(parts, axis=0)


def kernel(x_prompt, x_sample, state_ret, cache_kv_latent, cache_k_rope, page_table, norm_mix, norm_mlp, norm_final, w_ret_in, ret_gn_w, ret_gn_b, w_ret_out, w_mla_in, mla_q_norm, mla_kv_norm, w_mla_uq, w_mla_uk, w_mla_uv, w_mla_out, w_up, w_down):
    bp, seq, d = x_prompt.shape
    bs, ls, _ = x_sample.shape
    assert ls == 1, "the sample group is a single new token per sequence"
    depth = norm_mix.shape[0]
    assert depth >= 1, "the final norm is fused into the last layer's MLP"
    ret_heads, ret_dk, ret_dv = state_ret.shape[2:]
    qk_w = ret_heads * ret_dk
    v_w = ret_heads * ret_dv
    kv_lora, mla_heads, nope = w_mla_uk.shape[1:]
    v_dim = w_mla_uv.shape[3]
    q_lora = mla_q_norm.shape[1]
    rope_dim = cache_k_rope.shape[3]
    page = cache_kv_latent.shape[2]
    assert ret_dk == 2 * LANES and nope == LANES and v_dim == LANES and rope_dim * 2 == LANES
    past = page_table.shape[1] * page
    scale = float((nope + rope_dim) ** -0.5)

    pos_p = jnp.arange(seq, dtype=F32)
    pos_s = jnp.broadcast_to(past + jnp.arange(ls, dtype=F32), (bs,))
    ret_tab_p = _rope_cos_sin(pos_p, ret_dk // 2)
    ret_tab_s = _rope_cos_sin(pos_s, ret_dk // 2)
    mla_tab_p = _rope64_tables(pos_p)
    mla_tab_s = _rope64_tables(pos_s)

    xp = x_prompt.reshape(bp * seq, d)
    xs = x_sample.reshape(bs, d)
    ret_p, ret_s, ckv_p, kr_p, ckv_s, kr_s = [], [], [], [], [], []
    for i in range(depth):
        j = i // 2
        if i % 2 == 0:
            w_in = w_ret_in[j].astype(BF16)
            w_out = w_ret_out[j].astype(BF16)
            hq = _ret_in_proj(xp, norm_mix[i], w_in, *ret_tab_p, qk_w=qk_w, v_w=v_w, seq=seq, out_dtype=BF16)
            y, r_p = _ret_prompt(hq, ret_gn_w[j], ret_gn_b[j], batch=bp, seq=seq,
                                 heads=ret_heads, dk=ret_dk, dv=ret_dv)
            xp = _proj_residual(y, w_out, xp)
            hq = _ret_in_proj(xs, norm_mix[i], w_in, *ret_tab_s, qk_w=qk_w, v_w=v_w, seq=bs, out_dtype=F32)
            y, r_s = _ret_sample(hq, state_ret, ret_gn_w[j], ret_gn_b[j], layer=j,
                                 heads=ret_heads, dk=ret_dk, dv=ret_dv)
            xs = _proj_residual(y, w_out, xs)
            ret_p.append(r_p[None])
            ret_s.append(r_s)
        else:
            w_in = jnp.pad(w_mla_in[j], ((0, 0), (0, LANES - rope_dim))).astype(BF16)
            w_uq = w_mla_uq[j].reshape(q_lora, mla_heads, nope + rope_dim)
            w_uq = jnp.concatenate(
                [w_uq[:, :, :nope].reshape(q_lora, mla_heads * nope),
                 jnp.pad(w_uq[:, :, nope:], ((0, 0), (0, 0), (0, LANES - rope_dim))).reshape(q_lora, mla_heads * LANES)],
                axis=1).astype(BF16)
            w_uk2 = w_mla_uk[j].reshape(kv_lora, mla_heads * nope).astype(BF16)
            w_uv2 = w_mla_uv[j].reshape(kv_lora, mla_heads * v_dim).astype(BF16)
            w_out = w_mla_out[j].astype(BF16)
            nope_w = mla_heads * nope

            cq, ckv, ckvb, kr, krb = _mla_in_proj(xp, norm_mix[i], w_in, mla_q_norm[j], mla_kv_norm[j], mla_tab_p,
                                                  q_lora=q_lora, kv_lora=kv_lora, seq=seq)
            q = _mla_q_proj(cq, w_uq, mla_tab_p, nope_w=nope_w, scale=scale * LOG2_E, seq=seq, out_dtype=BF16)
            kn = _plain_matmul(ckvb, w_uk2, BF16)
            vt = _mla_v_expand_t(ckvb, w_uv2.T, _tile(seq, FLASH_TILE))
            o = _mla_prompt_attention(q, kn, krb, vt, batch=bp, seq=seq, heads=mla_heads)
            xp = _proj_residual(o, w_out, xp)
            ckv_p.append(ckv.reshape(1, bp, seq // page, page, kv_lora))
            kr_p.append(kr[:, :rope_dim].reshape(1, bp, seq // page, page, rope_dim))

            cq, ckv, ckvb, kr, krb = _mla_in_proj(xs, norm_mix[i], w_in, mla_q_norm[j], mla_kv_norm[j], mla_tab_s,
                                                  q_lora=q_lora, kv_lora=kv_lora, seq=bs)
            q = _mla_q_proj(cq, w_uq, mla_tab_s, nope_w=nope_w, scale=scale, seq=bs, out_dtype=F32)
            q_lat = _absorb_uk(q[:, :nope_w].astype(BF16), w_uk2, heads=mla_heads)
            o_lat = _mla_paged_attention(
                q_lat.transpose(1, 0, 2), q[:, nope_w:].reshape(bs, mla_heads, LANES), ckv, kr,
                cache_kv_latent, jnp.swapaxes(cache_k_rope, 2, 3), page_table, layer=j)
            o = _expand_uv(o_lat.transpose(1, 0, 2).astype(BF16), w_uv2, heads=mla_heads)
            xs = _proj_residual(o, w_out, xs)
            ckv_s.append(ckv.reshape(1, bs, ls, kv_lora))
            kr_s.append(kr[:, :rope_dim].reshape(1, bs, ls, rope_dim))
        w_u = w_up[i].astype(BF16)
        w_d = w_down[i].astype(BF16)
        final_gain = norm_final if i == depth - 1 else None
        xp = _mlp_residual(xp, norm_mlp[i], w_u, w_d, final_gain)
        xs = _mlp_residual(xs, norm_mlp[i], w_u, w_d, final_gain)
    y_prompt = xp.reshape(bp, seq, d)
    y_sample = xs.reshape(bs, ls, d)
    return (y_prompt, y_sample) + tuple(_concat_layers(t) for t in (ret_p, ret_s, ckv_p, kr_p, ckv_s, kr_s))
```

```python
import functools

import jax
import jax.numpy as jnp
from jax import lax
from jax.experimental import pallas as pl
from jax.experimental.pallas import tpu as pltpu

ROPE_BASE = 10000.0
NORM_EPS = 1e-6
GN_EPS = 1e-5
NEG = -1e30
RET_CHUNK = 256
LOG2_E = 1.4426950408889634

LANES = 128
BF16_SUBLANES = 16
V7X_VMEM_BYTES = 64 * 1024 * 1024
VMEM_LIMIT = 52 * 1024 * 1024

F32 = jnp.float32
BF16 = jnp.bfloat16


def _params(*sem):
    return pltpu.CompilerParams(dimension_semantics=sem, vmem_limit_bytes=VMEM_LIMIT)


def _tile(n, pref):
    if n <= pref:
        return n
    t = pref
    while n % t:
        t //= 2
    return t


def _nt_dot(a, b):
    return lax.dot_general(a, b, (((1,), (1,)), ((), ())), preferred_element_type=F32)


def _tn_dot(a, b):
    return lax.dot_general(a, b, (((0,), (0,)), ((), ())), preferred_element_type=F32)


def _rope_cos_sin(pos, d2):
    inv = ROPE_BASE ** (-jnp.arange(d2, dtype=F32) / d2)
    ang = pos[:, None] * inv[None, :]
    return jnp.cos(ang), jnp.sin(ang)


def _rope64_tables(pos):
    cos, sin = _rope_cos_sin(pos, 32)
    z = jnp.zeros_like(sin)
    cos_t = jnp.concatenate([cos, cos, cos, cos], axis=-1)
    sa = jnp.concatenate([-sin, z, -sin, z], axis=-1)
    sb = jnp.concatenate([z, sin, z, sin], axis=-1)
    return cos_t, sa, sb


def _rope64(x, cos_t, sa, sb):
    return x * cos_t + pltpu.roll(x, 96, 1) * sa + pltpu.roll(x, 32, 1) * sb


def _mm_kernel(*refs, has_norm, n_extra, epilogue):
    if has_norm:
        x_ref, g_ref, w_ref = refs[:3]
        rest = refs[3:]
        xn_ref = rest[-1]
        rest = rest[:-1]

        @pl.when(pl.program_id(1) == 0)
        def _():
            x = x_ref[...]
            ms = jnp.mean(x * x, axis=-1, keepdims=True)
            xn_ref[...] = (x * lax.rsqrt(ms + NORM_EPS) * g_ref[...]).astype(xn_ref.dtype)

        lhs = xn_ref[...]
    else:
        x_ref, w_ref = refs[:2]
        rest = refs[2:]
        lhs = x_ref[...]
    extra = rest[:n_extra]
    outs = rest[n_extra:]
    product = lambda: jnp.dot(lhs, w_ref[...], preferred_element_type=F32)
    epilogue(pl.program_id(1), product, extra, outs)


def _matmul(x, w, *, name, gain=None, extras=(), extra_specs=(), epilogue, out_shapes, out_specs, tm, tn):
    m, k = x.shape
    n = w.shape[1]
    has_norm = gain is not None
    in_specs = [pl.BlockSpec((tm, k), lambda i, j: (i, 0))]
    args = [x]
    if has_norm:
        in_specs.append(pl.BlockSpec((1, k), lambda i, j: (0, 0)))
        args.append(gain.reshape(1, k).astype(F32))
    in_specs.append(pl.BlockSpec((k, tn), lambda i, j: (0, j)))
    args.append(w)
    in_specs += list(extra_specs)
    args += list(extras)
    scratch = [pltpu.VMEM((tm, k), BF16)] if has_norm else []
    return pl.pallas_call(
        functools.partial(_mm_kernel, has_norm=has_norm, n_extra=len(extras), epilogue=epilogue),
        grid=(m // tm, n // tn),
        in_specs=in_specs,
        out_specs=out_specs,
        out_shape=out_shapes,
        scratch_shapes=scratch,
        compiler_params=_params("parallel", "arbitrary"),
        name=name,
    )(*args)


def _ret_in_epilogue(j, product, extra, outs, *, tn, qk_w, v_w):
    cos_ref, sin_ref = extra
    (o_ref,) = outs
    n_qk = (2 * qk_w) // tn
    n_v = v_w // tn

    @pl.when(j < n_qk)
    def _():
        acc = product()
        cos = cos_ref[...]
        sin = sin_ref[...]
        scale = jnp.where(j >= qk_w // tn, 0.0625, 1.0).astype(F32)
        for h in range(tn // 256):
            x1 = acc[:, h * 256:h * 256 + 128]
            x2 = acc[:, h * 256 + 128:(h + 1) * 256]
            o_ref[:, h * 256:h * 256 + 128] = ((x1 * cos - x2 * sin) * scale).astype(o_ref.dtype)
            o_ref[:, h * 256 + 128:(h + 1) * 256] = ((x2 * cos + x1 * sin) * scale).astype(o_ref.dtype)

    @pl.when(jnp.logical_and(j >= n_qk, j < n_qk + n_v))
    def _():
        o_ref[...] = product().astype(o_ref.dtype)

    @pl.when(j >= n_qk + n_v)
    def _():
        acc = product()
        o_ref[...] = (acc * jax.nn.sigmoid(acc)).astype(o_ref.dtype)


def _ret_in_proj(x, gain, w, cos, sin, *, qk_w, v_w, seq, out_dtype):
    m = x.shape[0]
    n = w.shape[1]
    tm = _tile(min(m, seq), 1024)
    tn = _tile(qk_w, 1024)
    nseq = seq // tm
    tab = pl.BlockSpec((tm, LANES), lambda i, j: (i % nseq, 0))
    (out,) = _matmul(
        x, w, name="ret_in_proj", gain=gain, extras=(cos, sin), extra_specs=(tab, tab),
        epilogue=functools.partial(_ret_in_epilogue, tn=tn, qk_w=qk_w, v_w=v_w),
        out_shapes=[jax.ShapeDtypeStruct((m, n), out_dtype)],
        out_specs=[pl.BlockSpec((tm, tn), lambda i, j: (i, j))],
        tm=tm, tn=tn)
    return out


def _ret_log_decay(heads):
    return jnp.log1p(-jnp.exp2(-5.0 - jnp.arange(heads, dtype=F32)))


def _group_norm_gate(o, gw, gb, gate):
    mu = jnp.mean(o, axis=-1, keepdims=True)
    d = o - mu
    var = jnp.mean(d * d, axis=-1, keepdims=True)
    return (d * lax.rsqrt(var + GN_EPS) * gw + gb) * gate


def _ret_chunk_kernel(h_ref, dec_ref, rs_ref, we_ref, gc_ref, gw_ref, gb_ref, y_ref, st_ref, r_ref,
                      *, heads, dk, dv):
    c = pl.program_id(1)
    qk_w = heads * dk
    v_w = heads * dv

    @pl.when(c == 0)
    def _():
        r_ref[...] = jnp.zeros_like(r_ref)

    for h in range(heads):
        q = h_ref[:, h * dk:(h + 1) * dk]
        k = h_ref[:, qk_w + h * dk:qk_w + (h + 1) * dk]
        v = h_ref[:, 2 * qk_w + h * dv:2 * qk_w + (h + 1) * dv]
        g = h_ref[:, 2 * qk_w + v_w + h * dv:2 * qk_w + v_w + (h + 1) * dv]
        r = r_ref[h]
        s = (_nt_dot(q, k) * dec_ref[h]).astype(BF16)
        rs = jnp.concatenate([rs_ref[h]] * (dv // LANES), axis=1)
        o = jnp.dot(s, v, preferred_element_type=F32)
        o = o + jnp.dot(q, r.astype(BF16), preferred_element_type=F32) * rs
        we = jnp.concatenate([we_ref[h]] * (dk // LANES), axis=1)
        kw = (k.astype(F32) * we).astype(BF16)
        gc = jnp.concatenate([gc_ref[h, 0:1, :]] * (dv // LANES), axis=1)
        r_ref[h] = r * gc + _tn_dot(kw, v)
        gw = gw_ref[:, h * dv:(h + 1) * dv]
        gb = gb_ref[:, h * dv:(h + 1) * dv]
        y_ref[:, h * dv:(h + 1) * dv] = _group_norm_gate(o, gw, gb, g.astype(F32)).astype(y_ref.dtype)

    @pl.when(c == pl.num_programs(1) - 1)
    def _():
        st_ref[0] = r_ref[...]


def _ret_prompt(hq, gn_w, gn_b, *, batch, seq, heads, dk, dv):
    chunk = RET_CHUNK
    nc = seq // chunk
    qk_w = heads * dk
    v_w = heads * dv
    log_g = _ret_log_decay(heads)
    n = jnp.arange(chunk, dtype=F32)
    diff = n[:, None] - n[None, :]
    causal = diff >= 0
    decay = jnp.where(causal[None], jnp.exp(jnp.where(causal, diff, 0.0)[None] * log_g[:, None, None]), 0.0)
    ones = jnp.ones((1, 1, LANES), F32)
    rowscale = jnp.exp((n + 1.0)[None, :] * log_g[:, None])[:, :, None] * ones
    w_end = jnp.exp((chunk - 1.0 - n)[None, :] * log_g[:, None])[:, :, None] * ones
    g_chunk = jnp.exp(chunk * log_g)[:, None, None] * jnp.ones((1, 8, LANES), F32)

    tab = lambda shape: pl.BlockSpec(shape, lambda b, c: (0, 0, 0))
    vec = pl.BlockSpec((1, v_w), lambda b, c: (0, 0))
    y, state = pl.pallas_call(
        functools.partial(_ret_chunk_kernel, heads=heads, dk=dk, dv=dv),
        grid=(batch, nc),
        in_specs=[
            pl.BlockSpec((chunk, hq.shape[1]), lambda b, c: (b * nc + c, 0)),
            tab((heads, chunk, chunk)), tab((heads, chunk, LANES)), tab((heads, chunk, LANES)),
            tab((heads, 8, LANES)), vec, vec,
        ],
        out_specs=[
            pl.BlockSpec((chunk, v_w), lambda b, c: (b * nc + c, 0)),
            pl.BlockSpec((1, heads, dk, dv), lambda b, c: (b, 0, 0, 0)),
        ],
        out_shape=[
            jax.ShapeDtypeStruct((batch * seq, v_w), BF16),
            jax.ShapeDtypeStruct((batch, heads, dk, dv), F32),
        ],
        scratch_shapes=[pltpu.VMEM((heads, dk, dv), F32)],
        compiler_params=_params("parallel", "arbitrary"),
        name="ret_chunk_scan",
    )(hq, decay, rowscale, w_end, g_chunk,
      gn_w.reshape(1, v_w).astype(F32), gn_b.reshape(1, v_w).astype(F32))
    return y, state


def _column(row):
    n = row.shape[1]
    eye = lax.broadcasted_iota(jnp.int32, (n, n), 0) == lax.broadcasted_iota(jnp.int32, (n, n), 1)
    return jnp.sum(jnp.where(eye, row, 0.0), axis=1, keepdims=True)


def _ret_sample_heads(gam_ref, head0, q_ref, k_ref, v_ref, g_ref, st_ref, gw_ref, gb_ref, y_ref, ns_ref,
                      *, hb, dk, dv):
    for h in range(hb):
        gam = gam_ref[head0 + h]
        q = q_ref[0, :, h * dk:(h + 1) * dk]
        k = k_ref[0, :, h * dk:(h + 1) * dk]
        v = v_ref[0, :, h * dv:(h + 1) * dv]
        g = g_ref[0, :, h * dv:(h + 1) * dv]
        r = st_ref[0, 0, h]
        s = jnp.sum(q * k, axis=-1, keepdims=True)
        o = s * v + jnp.sum(r * _column(q), axis=0, keepdims=True) * gam
        ns_ref[0, 0, h] = r * gam + _column(k) * v
        gw = gw_ref[:, h * dv:(h + 1) * dv]
        gb = gb_ref[:, h * dv:(h + 1) * dv]
        y_ref[0, :, h * dv:(h + 1) * dv] = _group_norm_gate(o, gw, gb, g).astype(y_ref.dtype)


def _ret_sample_io(unit, hq, state, gn_w, gn_b, *, layer, heads, hb, dk, dv):
    batch, n = hq.shape
    qk_w = heads * dk
    v_w = heads * dv

    def col(width, base):
        return pl.BlockSpec((1, 1, width), lambda *idx: (unit(*idx)[0], 0, base + unit(*idx)[1]))

    def st(lead):
        return pl.BlockSpec((1, 1, hb, dk, dv), lambda *idx: (lead, unit(*idx)[0], unit(*idx)[1], 0, 0))

    vec = pl.BlockSpec((1, hb * dv), lambda *idx: (0, unit(*idx)[1]))
    in_specs = [pl.BlockSpec(memory_space=pltpu.SMEM),
                col(hb * dk, 0), col(hb * dk, heads // hb),
                col(hb * dv, (2 * qk_w) // (hb * dv)), col(hb * dv, (2 * qk_w + v_w) // (hb * dv)),
                st(layer), vec, vec]
    out_specs = [pl.BlockSpec((1, 1, hb * dv), lambda *idx: (unit(*idx)[0], 0, unit(*idx)[1])), st(0)]
    hq3 = hq.reshape(batch, 1, n)
    args = [jnp.exp(1.0 * _ret_log_decay(heads)), hq3, hq3, hq3, hq3, state,
            gn_w.reshape(1, v_w).astype(F32), gn_b.reshape(1, v_w).astype(F32)]
    out_shapes = [jax.ShapeDtypeStruct((batch, 1, v_w), BF16),
                  jax.ShapeDtypeStruct((1,) + state.shape[1:], F32)]
    return in_specs, args, out_specs, out_shapes


def _ret_sample_kernel(gam_ref, *refs, heads, dk, dv):
    _ret_sample_heads(gam_ref, 0, *refs, hb=heads, dk=dk, dv=dv)


def _ret_sample(hq, state, gn_w, gn_b, *, layer, heads, dk, dv):
    batch = hq.shape[0]
    in_specs, args, out_specs, out_shapes = _ret_sample_io(
        lambda b: (b, 0), hq, state, gn_w, gn_b, layer=layer, heads=heads, hb=heads, dk=dk, dv=dv)
    y, new_state = pl.pallas_call(
        functools.partial(_ret_sample_kernel, heads=heads, dk=dk, dv=dv),
        grid=(batch,),
        in_specs=in_specs,
        out_specs=out_specs,
        out_shape=out_shapes,
        compiler_params=_params("parallel"),
        name="ret_sample_step",
    )(*args)
    return y.reshape(batch, heads * dv), new_state


def _residual_epilogue(j, product, extra, outs):
    (res_ref,) = extra
    (o_ref,) = outs
    o_ref[...] = res_ref[...] + product()


def _proj_residual(a, w, res):
    m = a.shape[0]
    n = w.shape[1]
    tm = _tile(m, 512)
    tn = _tile(n, 1024)
    blk = pl.BlockSpec((tm, tn), lambda i, j: (i, j))
    (out,) = _matmul(a, w, name="proj_residual", extras=(res,), extra_specs=(blk,), epilogue=_residual_epilogue,
                     out_shapes=[jax.ShapeDtypeStruct((m, n), F32)], out_specs=[blk], tm=tm, tn=tn)
    return out


def _mlp_kernel(x_ref, g_ref, wu_ref, wd_ref, *rest, final_norm, rider):
    rest = list(rest)
    fg_ref = rest.pop(0) if final_norm else None
    ride_in = [rest.pop(0) for _ in range(8)] if rider else None
    o_ref = rest.pop(0)
    ride_out = [rest.pop(0) for _ in range(2)] if rider else None
    (xn_ref,) = rest
    f = pl.program_id(1)

    @pl.when(f == 0)
    def _():
        x = x_ref[...]
        ms = jnp.mean(x * x, axis=-1, keepdims=True)
        xn_ref[...] = (x * lax.rsqrt(ms + NORM_EPS) * g_ref[...]).astype(xn_ref.dtype)
        o_ref[...] = x

    h = jnp.dot(xn_ref[...], wu_ref[0], preferred_element_type=F32)
    if rider:
        hb, dk, dv, n_split = rider
        unit = pl.program_id(0) * pl.num_programs(1) + f
        _ret_sample_heads(ride_in[0], lax.rem(unit, n_split) * hb, *ride_in[1:], *ride_out, hb=hb, dk=dk, dv=dv)
    h = jnp.square(jnp.maximum(h, 0.0)).astype(BF16)
    o_ref[...] += jnp.dot(h, wd_ref[0], preferred_element_type=F32)

    if final_norm:
        @pl.when(f == pl.num_programs(1) - 1)
        def _():
            y = o_ref[...]
            ms = jnp.mean(y * y, axis=-1, keepdims=True)
            o_ref[...] = y * lax.rsqrt(ms + NORM_EPS) * fg_ref[...]


def _mlp_residual(x, gain, w_up, w_down, layer, final_gain=None, ret_sample=None):
    m, d = x.shape
    ff = w_up.shape[2]
    tm = _tile(m, 512)
    tf = _tile(ff, 1024)
    grid = (m // tm, ff // tf)
    final_norm = final_gain is not None
    vec = pl.BlockSpec((1, d), lambda i, f: (0, 0))
    in_specs = [
        pl.BlockSpec((tm, d), lambda i, f: (i, 0)),
        vec,
        pl.BlockSpec((1, d, tf), lambda i, f: (layer, 0, f)),
        pl.BlockSpec((1, tf, d), lambda i, f: (layer, f, 0)),
    ]
    args = [x, gain.reshape(1, d).astype(F32), w_up, w_down]
    if final_norm:
        in_specs.append(vec)
        args.append(final_gain.reshape(1, d).astype(F32))
    out_specs = [pl.BlockSpec((tm, d), lambda i, f: (i, 0))]
    out_shapes = [jax.ShapeDtypeStruct((m, d), F32)]
    rider = None
    if ret_sample is not None:
        r = ret_sample
        batch = r["hq"].shape[0]
        n_split, rem = divmod(grid[0] * grid[1], batch)
        if rem == 0 and n_split >= 1 and r["heads"] % n_split == 0:
            hb = r["heads"] // n_split
            unit = lambda i, f: ((i * grid[1] + f) // n_split, (i * grid[1] + f) % n_split)
            r_in, r_args, r_out, r_shapes = _ret_sample_io(
                unit, r["hq"], r["state"], r["gn_w"], r["gn_b"],
                layer=r["layer"], heads=r["heads"], hb=hb, dk=r["dk"], dv=r["dv"])
            in_specs += r_in
            args += r_args
            out_specs += r_out
            out_shapes += r_shapes
            rider = (hb, r["dk"], r["dv"], n_split)
    outs = pl.pallas_call(
        functools.partial(_mlp_kernel, final_norm=final_norm, rider=rider),
        grid=grid,
        in_specs=in_specs,
        out_specs=out_specs,
        out_shape=out_shapes,
        scratch_shapes=[pltpu.VMEM((tm, d), BF16)],
        compiler_params=_params("arbitrary" if rider else "parallel", "arbitrary"),
        name="mlp_residual_ret_sample" if rider else "mlp_residual",
    )(*args)
    if ret_sample is None:
        return outs[0]
    if rider is None:
        return outs[0], None, None
    return outs[0], outs[1].reshape(outs[1].shape[0], -1), outs[2]


def _mla_in_epilogue(j, product, extra, outs, *, q_lora, kv_lora):
    qg_ref, kg_ref, cos_ref, sa_ref, sb_ref = extra
    cq_ref, ckv_ref, ckvb_ref, kr_ref, krb_ref = outs
    acc = product()

    def norm(x, g):
        return x * lax.rsqrt(jnp.mean(x * x, axis=-1, keepdims=True) + NORM_EPS) * g

    cq_ref[...] = norm(acc[:, :q_lora], qg_ref[...]).astype(cq_ref.dtype)
    ckv = norm(acc[:, q_lora:q_lora + kv_lora], kg_ref[...])
    ckv_ref[...] = ckv
    ckvb_ref[...] = ckv.astype(ckvb_ref.dtype)
    kr = _rope64(acc[:, q_lora + kv_lora:], cos_ref[...], sa_ref[...], sb_ref[...])
    kr_ref[...] = kr
    krb_ref[...] = kr.astype(krb_ref.dtype)


def _mla_in_proj(x, gain, w, q_gain, kv_gain, tabs, *, q_lora, kv_lora, seq):
    m = x.shape[0]
    n = w.shape[1]
    tm = _tile(min(m, seq), 512)
    nseq = seq // tm
    tab = pl.BlockSpec((tm, LANES), lambda i, j: (i % nseq, 0))
    vec = lambda width: pl.BlockSpec((1, width), lambda i, j: (0, 0))
    blk = lambda width: pl.BlockSpec((tm, width), lambda i, j: (i, 0))
    return _matmul(
        x, w, name="mla_in_proj", gain=gain,
        extras=(q_gain.reshape(1, q_lora).astype(F32), kv_gain.reshape(1, kv_lora).astype(F32)) + tuple(tabs),
        extra_specs=(vec(q_lora), vec(kv_lora), tab, tab, tab),
        epilogue=functools.partial(_mla_in_epilogue, q_lora=q_lora, kv_lora=kv_lora),
        out_shapes=[jax.ShapeDtypeStruct((m, q_lora), BF16),
                    jax.ShapeDtypeStruct((m, kv_lora), F32),
                    jax.ShapeDtypeStruct((m, kv_lora), BF16),
                    jax.ShapeDtypeStruct((m, LANES), F32),
                    jax.ShapeDtypeStruct((m, LANES), BF16)],
        out_specs=[blk(q_lora), blk(kv_lora), blk(kv_lora), blk(LANES), blk(LANES)],
        tm=tm, tn=n)


def _mla_q_epilogue(j, product, extra, outs, *, n_nope, scale):
    cos_ref, sa_ref, sb_ref = extra
    (o_ref,) = outs

    @pl.when(j < n_nope)
    def _():
        o_ref[...] = (product() * scale).astype(o_ref.dtype)

    @pl.when(j >= n_nope)
    def _():
        acc = product()
        cos = cos_ref[...]
        sa = sa_ref[...]
        sb = sb_ref[...]
        for h in range(acc.shape[1] // LANES):
            x = acc[:, h * LANES:(h + 1) * LANES]
            o_ref[:, h * LANES:(h + 1) * LANES] = (_rope64(x, cos, sa, sb) * scale).astype(o_ref.dtype)


def _mla_q_proj(cq, w, tabs, *, nope_w, scale, seq, out_dtype):
    m = cq.shape[0]
    n = w.shape[1]
    tm = _tile(min(m, seq), 1024)
    tn = _tile(nope_w, 1024)
    nseq = seq // tm
    tab = pl.BlockSpec((tm, LANES), lambda i, j: (i % nseq, 0))
    (out,) = _matmul(
        cq, w, name="mla_q_proj", extras=tuple(tabs), extra_specs=(tab, tab, tab),
        epilogue=functools.partial(_mla_q_epilogue, n_nope=nope_w // tn, scale=scale),
        out_shapes=[jax.ShapeDtypeStruct((m, n), out_dtype)],
        out_specs=[pl.BlockSpec((tm, tn), lambda i, j: (i, j))],
        tm=tm, tn=tn)
    return out


def _cast_epilogue(j, product, extra, outs):
    (o_ref,) = outs
    o_ref[...] = product().astype(o_ref.dtype)


def _plain_matmul(a, w, out_dtype):
    m = a.shape[0]
    n = w.shape[1]
    tm = _tile(m, 1024)
    tn = _tile(n, 1024)
    (out,) = _matmul(a, w, name="mla_k_expand", epilogue=_cast_epilogue,
                     out_shapes=[jax.ShapeDtypeStruct((m, n), out_dtype)],
                     out_specs=[pl.BlockSpec((tm, tn), lambda i, j: (i, j))], tm=tm, tn=tn)
    return out


FLASH_TILE = 512
FLASH_HEADS_PER_STEP = 4


def _vt_kernel(x_ref, wt_ref, o_ref):
    o_ref[0] = _nt_dot(wt_ref[...], x_ref[...]).astype(o_ref.dtype)


def _mla_v_expand_t(ckvb, w_uv_t, tq):
    m, c = ckvb.shape
    n = w_uv_t.shape[0]
    tn = _tile(n, 1024)
    return pl.pallas_call(
        _vt_kernel,
        grid=(m // tq, n // tn),
        in_specs=[pl.BlockSpec((tq, c), lambda i, j: (i, 0)), pl.BlockSpec((tn, c), lambda i, j: (j, 0))],
        out_specs=pl.BlockSpec((1, tn, tq), lambda i, j: (i, j, 0)),
        out_shape=jax.ShapeDtypeStruct((m // tq, n, tq), BF16),
        compiler_params=_params("parallel", "arbitrary"),
        name="mla_v_expand_t",
    )(ckvb, w_uv_t)


def _flash_kernel(qn_ref, qr_ref, kn_ref, kr_ref, vt_ref, o_ref, *, tq, hps):
    qi = pl.program_id(2)
    dv = vt_ref.shape[1] // hps
    qs = [jnp.concatenate([qn_ref[:, h * LANES:(h + 1) * LANES], qr_ref[:, h * LANES:(h + 1) * LANES]], axis=1)
          for h in range(hps)]

    def scores(h, start):
        k = jnp.concatenate([kn_ref[pl.ds(start, tq), h * LANES:(h + 1) * LANES], kr_ref[pl.ds(start, tq), :]],
                            axis=1)
        return _nt_dot(k, qs[h])

    ones = jnp.ones((BF16_SUBLANES, tq), BF16)

    def probs(carry, s):
        m, _ = carry
        m_new = jnp.maximum(m, jnp.max(s, axis=0, keepdims=True))
        return m_new, jnp.exp2(m - m_new), jnp.exp2(s - m_new).astype(BF16)

    def update_all(carries, s_all, kb):
        st = [probs(carries[h], s_all[h]) for h in range(hps)]
        out = []
        for h in range(hps):
            m_new, corr, pb = st[h]
            vt = jnp.concatenate([vt_ref[kb, h * dv:(h + 1) * dv, :], ones], axis=0)
            out.append((m_new, carries[h][1] * corr + jnp.dot(vt, pb, preferred_element_type=F32)))
        return tuple(out)

    def body(kb, carries):
        start = pl.multiple_of(kb * tq, tq)
        return update_all(carries, [scores(h, start) for h in range(hps)], kb)

    init = tuple((jnp.full((1, tq), NEG, F32), jnp.zeros((dv + BF16_SUBLANES, tq), F32)) for _ in range(hps))
    carries = lax.fori_loop(0, qi, body, init)
    start = pl.multiple_of(qi * tq, tq)
    causal = lax.broadcasted_iota(jnp.int32, (tq, tq), 0) <= lax.broadcasted_iota(jnp.int32, (tq, tq), 1)
    final = update_all(carries, [jnp.where(causal, scores(h, start), NEG) for h in range(hps)], qi)
    for h in range(hps):
        acc = final[h][1]
        o_ref[:, h * dv:(h + 1) * dv] = (acc[:dv] / acc[dv:dv + 1]).T.astype(o_ref.dtype)


def _mla_prompt_attention(q, kn, krb, vt, *, batch, seq, heads):
    m = q.shape[0]
    tq = vt.shape[2]
    nq = seq // tq
    hps = FLASH_HEADS_PER_STEP if heads % FLASH_HEADS_PER_STEP == 0 else 1
    w = hps * LANES
    ng = heads // hps
    qblk = lambda off: pl.BlockSpec((tq, w), lambda b, h, i: (b * nq + i, off + h))
    return pl.pallas_call(
        functools.partial(_flash_kernel, tq=tq, hps=hps),
        grid=(batch, ng, nq),
        in_specs=[qblk(0), qblk(ng),
                  pl.BlockSpec((seq, w), lambda b, h, i: (b, h)),
                  pl.BlockSpec((seq, LANES), lambda b, h, i: (b, 0)),
                  pl.BlockSpec((nq, w, tq), lambda b, h, i: (b, h, 0))],
        out_specs=pl.BlockSpec((tq, w), lambda b, h, i: (b * nq + i, h)),
        out_shape=jax.ShapeDtypeStruct((m, heads * LANES), BF16),
        compiler_params=_params("parallel", "parallel", "arbitrary"),
        name="mla_prompt_flash",
    )(q, q, kn, krb, vt)


def _head_nt_kernel(a_ref, w_ref, o_ref):
    o_ref[0] = _nt_dot(a_ref[...], w_ref[...]).astype(o_ref.dtype)


def _absorb_uk(qn, w_uk2, *, heads):
    b = qn.shape[0]
    c = w_uk2.shape[0]
    return pl.pallas_call(
        _head_nt_kernel,
        grid=(heads,),
        in_specs=[pl.BlockSpec((b, LANES), lambda h: (0, h)), pl.BlockSpec((c, LANES), lambda h: (0, h))],
        out_specs=pl.BlockSpec((1, b, c), lambda h: (h, 0, 0)),
        out_shape=jax.ShapeDtypeStruct((heads, b, c), F32),
        compiler_params=_params("parallel"),
        name="mla_absorb_uk",
    )(qn, w_uk2)


def _head_nn_kernel(a_ref, w_ref, o_ref):
    o_ref[...] = jnp.dot(a_ref[0], w_ref[...], preferred_element_type=F32).astype(o_ref.dtype)


def _expand_uv(o_lat, w_uv2, *, heads):
    b = o_lat.shape[1]
    c = o_lat.shape[2]
    return pl.pallas_call(
        _head_nn_kernel,
        grid=(heads,),
        in_specs=[pl.BlockSpec((1, b, c), lambda h: (h, 0, 0)), pl.BlockSpec((c, LANES), lambda h: (0, h))],
        out_specs=pl.BlockSpec((b, LANES), lambda h: (0, h)),
        out_shape=jax.ShapeDtypeStruct((b, heads * LANES), BF16),
        compiler_params=_params("parallel"),
        name="mla_expand_uv",
    )(o_lat, w_uv2)


def _paged_kernel(pt_ref, ql_ref, qr_ref, cn_ref, kn_ref, ckv_hbm, kr_hbm, o_ref,
                  ckv_buf, kr_buf, sem, m_ref, l_ref, acc_ref, *, layer, sps, pps, rope_dim):
    n_groups = pl.num_programs(1)
    g = pl.program_id(1)
    step = pl.program_id(0) * n_groups + g
    n_steps = pl.num_programs(0) * n_groups
    slot = lax.rem(step, 2)

    def gather(t, buf_slot):
        row0 = (t // n_groups) * sps
        col0 = lax.rem(t, n_groups) * pps
        out = []
        for s in range(sps):
            for k in range(pps):
                pid = pt_ref[row0 + s, col0 + k]
                j = s * pps + k
                out.append(pltpu.make_async_copy(ckv_hbm.at[layer, pid], ckv_buf.at[buf_slot, j], sem.at[0, buf_slot]))
                out.append(pltpu.make_async_copy(kr_hbm.at[layer, pid], kr_buf.at[buf_slot, j], sem.at[1, buf_slot]))
        return out

    @pl.when(step == 0)
    def _():
        for c in gather(step, slot):
            c.start()

    @pl.when(step + 1 < n_steps)
    def _():
        for c in gather(step + 1, 1 - slot):
            c.start()

    for c in gather(step, slot):
        c.wait()

    @pl.when(g == 0)
    def _():
        m_ref[...] = jnp.full_like(m_ref, NEG)
        l_ref[...] = jnp.zeros_like(l_ref)
        acc_ref[...] = jnp.zeros_like(acc_ref)

    ql = [ql_ref[s] for s in range(sps)]
    qr = [qr_ref[s][:, :rope_dim] for s in range(sps)]
    pages = [[ckv_buf[slot, s * pps + i].astype(BF16) for i in range(pps)] for s in range(sps)]
    page = pages[0][0].shape[0]
    scores = []
    for s in range(sps):
        qlb = ql[s].astype(BF16)
        qrb = qr[s].astype(BF16)
        scores.append(jnp.concatenate(
            [_nt_dot(qlb, pages[s][i])
             + jnp.dot(qrb, kr_buf[slot, s * pps + i].astype(BF16), preferred_element_type=F32)
             for i in range(pps)], axis=1))
    stats = []
    for s in range(sps):
        m = m_ref[s]
        m_new = jnp.maximum(m, jnp.max(scores[s], axis=-1, keepdims=True))
        corr = jnp.exp(m - m_new)
        p = jnp.exp(scores[s] - m_new)
        l_new = l_ref[s] * corr + jnp.sum(p, axis=-1, keepdims=True)
        stats.append((m_new, corr, l_new, p.astype(BF16)))
    accs = []
    for s in range(sps):
        m_new, corr, l_new, pb = stats[s]
        pv = jnp.dot(pb[:, :page], pages[s][0], preferred_element_type=F32)
        for i in range(1, pps):
            pv = pv + jnp.dot(pb[:, i * page:(i + 1) * page], pages[s][i], preferred_element_type=F32)
        acc_new = acc_ref[s] * corr + pv
        m_ref[s] = m_new
        l_ref[s] = l_new
        acc_ref[s] = acc_new
        accs.append(acc_new)

    @pl.when(g == pl.num_programs(1) - 1)
    def _():
        for s in range(sps):
            m_new, _, l_new, _ = stats[s]
            cn = cn_ref[s]
            s_new = (jnp.sum(ql[s] * cn, axis=-1, keepdims=True)
                     + jnp.sum(qr[s] * kn_ref[s][:, :rope_dim], axis=-1, keepdims=True))
            m_fin = jnp.maximum(m_new, s_new)
            c_fin = jnp.exp(m_new - m_fin)
            p_new = jnp.exp(s_new - m_fin)
            l_fin = l_new * c_fin + p_new
            o_ref[s] = (accs[s] * c_fin + p_new * cn) / l_fin


PAGES_PER_STEP = 16
SEQS_PER_STEP = 2


def _mla_paged_attention(q_lat, qr, ckv_new, kr_new, cache_ckv, cache_kr_t, page_table, *, layer):
    b, heads, c = q_lat.shape
    n_pages = page_table.shape[1]
    page = cache_ckv.shape[2]
    rope_dim = cache_kr_t.shape[2]
    pps = _tile(n_pages, PAGES_PER_STEP)
    sps = SEQS_PER_STEP if b % SEQS_PER_STEP == 0 else 1
    per_b = lambda width: pl.BlockSpec((sps, heads, width), lambda i, g, pt: (i, 0, 0))
    new_b = lambda width: pl.BlockSpec((sps, 1, width), lambda i, g, pt: (i, 0, 0))

    hbm = pl.BlockSpec(memory_space=pl.ANY)
    n_slots = sps * pps
    grid_spec = pltpu.PrefetchScalarGridSpec(
        num_scalar_prefetch=1,
        grid=(b // sps, n_pages // pps),
        in_specs=[per_b(c), per_b(LANES), new_b(c), new_b(LANES), hbm, hbm],
        out_specs=per_b(c),
        scratch_shapes=[pltpu.VMEM((2, n_slots, page, c), cache_ckv.dtype),
                        pltpu.VMEM((2, n_slots, rope_dim, page), cache_kr_t.dtype),
                        pltpu.SemaphoreType.DMA((2, 2)),
                        pltpu.VMEM((sps, heads, 1), F32), pltpu.VMEM((sps, heads, 1), F32),
                        pltpu.VMEM((sps, heads, c), F32)],
    )
    return pl.pallas_call(
        functools.partial(_paged_kernel, layer=layer, sps=sps, pps=pps, rope_dim=rope_dim),
        grid_spec=grid_spec,
        out_shape=jax.ShapeDtypeStruct((b, heads, c), F32),
        compiler_params=_params("arbitrary", "arbitrary"),
        name="mla_paged_attention",
    )(page_table, q_lat, qr, ckv_new.reshape(b, 1, c), kr_new.reshape(b, 1, LANES), cache_ckv, cache_kr_t)


def _concat_layers(parts):
    return parts[0] if len(parts) == 1 else jnp.concatenate(parts, axis=0)


def kernel(x_prompt, x_sample, state_ret, cache_kv_latent, cache_k_rope, page_table, norm_mix, norm_mlp, norm_final, w_ret_in, ret_gn_w, ret_gn_b, w_ret_out, w_mla_in, mla_q_norm, mla_kv_norm, w_mla_uq, w_mla_uk, w_mla_uv, w_mla_out, w_up, w_down):
    bp, seq, d = x_prompt.shape
    bs, ls, _ = x_sample.shape
    assert ls == 1, "the sample group is a single new token per sequence"
    depth = norm_mix.shape[0]
    assert depth >= 1, "the final norm is fused into the last layer's MLP"
    ret_heads, ret_dk, ret_dv = state_ret.shape[2:]
    qk_w = ret_heads * ret_dk
    v_w = ret_heads * ret_dv
    kv_lora, mla_heads, nope = w_mla_uk.shape[1:]
    v_dim = w_mla_uv.shape[3]
    q_lora = mla_q_norm.shape[1]
    rope_dim = cache_k_rope.shape[3]
    page = cache_kv_latent.shape[2]
    assert ret_dk == 2 * LANES and nope == LANES and v_dim == LANES and rope_dim * 2 == LANES
    past = page_table.shape[1] * page
    scale = float((nope + rope_dim) ** -0.5)

    pos_p = jnp.arange(seq, dtype=F32)
    pos_s = jnp.broadcast_to(past + jnp.arange(ls, dtype=F32), (bs,))
    ret_tab_p = _rope_cos_sin(pos_p, ret_dk // 2)
    ret_tab_s = _rope_cos_sin(pos_s, ret_dk // 2)
    mla_tab_p = _rope64_tables(pos_p)
    mla_tab_s = _rope64_tables(pos_s)

    xp = x_prompt.reshape(bp * seq, d)
    xs = x_sample.reshape(bs, d)
    w_up_b = w_up.astype(BF16)
    w_down_b = w_down.astype(BF16)
    ret_p, ret_s, ckv_p, kr_p, ckv_s, kr_s = [], [], [], [], [], []
    for i in range(depth):
        j = i // 2
        ride = None
        if i % 2 == 0:
            w_in = w_ret_in[j].astype(BF16)
            w_out = w_ret_out[j].astype(BF16)
            hq = _ret_in_proj(xp, norm_mix[i], w_in, *ret_tab_p, qk_w=qk_w, v_w=v_w, seq=seq, out_dtype=BF16)
            y, r_p = _ret_prompt(hq, ret_gn_w[j], ret_gn_b[j], batch=bp, seq=seq,
                                 heads=ret_heads, dk=ret_dk, dv=ret_dv)
            xp = _proj_residual(y, w_out, xp)
            hq = _ret_in_proj(xs, norm_mix[i], w_in, *ret_tab_s, qk_w=qk_w, v_w=v_w, seq=bs, out_dtype=F32)
            ride = dict(hq=hq, state=state_ret, gn_w=ret_gn_w[j], gn_b=ret_gn_b[j], layer=j,
                        heads=ret_heads, dk=ret_dk, dv=ret_dv)
            ret_p.append(r_p[None])
        else:
            w_in = jnp.pad(w_mla_in[j], ((0, 0), (0, LANES - rope_dim))).astype(BF16)
            w_uq = w_mla_uq[j].reshape(q_lora, mla_heads, nope + rope_dim)
            w_uq = jnp.concatenate(
                [w_uq[:, :, :nope].reshape(q_lora, mla_heads * nope),
                 jnp.pad(w_uq[:, :, nope:], ((0, 0), (0, 0), (0, LANES - rope_dim))).reshape(q_lora, mla_heads * LANES)],
                axis=1).astype(BF16)
            w_uk2 = w_mla_uk[j].reshape(kv_lora, mla_heads * nope).astype(BF16)
            w_uv2 = w_mla_uv[j].reshape(kv_lora, mla_heads * v_dim).astype(BF16)
            w_out = w_mla_out[j].astype(BF16)
            nope_w = mla_heads * nope

            cq, ckv, ckvb, kr, krb = _mla_in_proj(xp, norm_mix[i], w_in, mla_q_norm[j], mla_kv_norm[j], mla_tab_p,
                                                  q_lora=q_lora, kv_lora=kv_lora, seq=seq)
            q = _mla_q_proj(cq, w_uq, mla_tab_p, nope_w=nope_w, scale=scale * LOG2_E, seq=seq, out_dtype=BF16)
            kn = _plain_matmul(ckvb, w_uk2, BF16)
            vt = _mla_v_expand_t(ckvb, w_uv2.T, _tile(seq, FLASH_TILE))
            o = _mla_prompt_attention(q, kn, krb, vt, batch=bp, seq=seq, heads=mla_heads)
            xp = _proj_residual(o, w_out, xp)
            ckv_p.append(ckv.reshape(1, bp, seq // page, page, kv_lora))
            kr_p.append(kr[:, :rope_dim].reshape(1, bp, seq // page, page, rope_dim))

            cq, ckv, ckvb, kr, krb = _mla_in_proj(xs, norm_mix[i], w_in, mla_q_norm[j], mla_kv_norm[j], mla_tab_s,
                                                  q_lora=q_lora, kv_lora=kv_lora, seq=bs)
            q = _mla_q_proj(cq, w_uq, mla_tab_s, nope_w=nope_w, scale=scale, seq=bs, out_dtype=F32)
            q_lat = _absorb_uk(q[:, :nope_w].astype(BF16), w_uk2, heads=mla_heads)
            o_lat = _mla_paged_attention(
                q_lat.transpose(1, 0, 2), q[:, nope_w:].reshape(bs, mla_heads, LANES), ckv, kr,
                cache_kv_latent, jnp.swapaxes(cache_k_rope, 2, 3), page_table, layer=j)
            o = _expand_uv(o_lat.transpose(1, 0, 2).astype(BF16), w_uv2, heads=mla_heads)
            xs = _proj_residual(o, w_out, xs)
            ckv_s.append(ckv.reshape(1, bs, ls, kv_lora))
            kr_s.append(kr[:, :rope_dim].reshape(1, bs, ls, rope_dim))
        final_gain = norm_final if i == depth - 1 else None
        if ride is None:
            xp = _mlp_residual(xp, norm_mlp[i], w_up_b, w_down_b, i, final_gain)
        else:
            xp, y, r_s = _mlp_residual(xp, norm_mlp[i], w_up_b, w_down_b, i, final_gain, ret_sample=ride)
            if y is None:
                y, r_s = _ret_sample(ride["hq"], state_ret, ride["gn_w"], ride["gn_b"], layer=j,
                                     heads=ret_heads, dk=ret_dk, dv=ret_dv)
            xs = _proj_residual(y, w_out, xs)
            ret_s.append(r_s)
        xs = _mlp_residual(xs, norm_mlp[i], w_up_b, w_down_b, i, final_gain)
    y_prompt = xp.reshape(bp, seq, d)
    y_sample = xs.reshape(bs, ls, d)
    return (y_prompt, y_sample) + tuple(_concat_layers(t) for t in (ret_p, ret_s, ckv_p, kr_p, ckv_s, kr_s))
```

```python
import functools

import jax
import jax.numpy as jnp
from jax import lax
from jax.experimental import pallas as pl
from jax.experimental.pallas import tpu as pltpu

ROPE_BASE = 10000.0
NORM_EPS = 1e-6
GN_EPS = 1e-5
NEG = -1e30
RET_CHUNK = 256
LOG2_E = 1.4426950408889634

LANES = 128
BF16_SUBLANES = 16
V7X_VMEM_BYTES = 64 * 1024 * 1024
VMEM_LIMIT = 52 * 1024 * 1024

F32 = jnp.float32
BF16 = jnp.bfloat16


def _params(*sem):
    return pltpu.CompilerParams(dimension_semantics=sem, vmem_limit_bytes=VMEM_LIMIT)


def _tile(n, pref):
    if n <= pref:
        return n
    t = pref
    while n % t:
        t //= 2
    return t


def _nt_dot(a, b):
    return lax.dot_general(a, b, (((1,), (1,)), ((), ())), preferred_element_type=F32)


def _tn_dot(a, b):
    return lax.dot_general(a, b, (((0,), (0,)), ((), ())), preferred_element_type=F32)


def _rope_cos_sin(pos, d2):
    inv = ROPE_BASE ** (-jnp.arange(d2, dtype=F32) / d2)
    ang = pos[:, None] * inv[None, :]
    return jnp.cos(ang), jnp.sin(ang)


def _rope64_tables(pos):
    cos, sin = _rope_cos_sin(pos, 32)
    z = jnp.zeros_like(sin)
    cos_t = jnp.concatenate([cos, cos, cos, cos], axis=-1)
    sa = jnp.concatenate([-sin, z, -sin, z], axis=-1)
    sb = jnp.concatenate([z, sin, z, sin], axis=-1)
    return cos_t, sa, sb


def _rope64(x, cos_t, sa, sb):
    return x * cos_t + pltpu.roll(x, 96, 1) * sa + pltpu.roll(x, 32, 1) * sb


def _mm_kernel(*refs, has_norm, n_extra, epilogue):
    if has_norm:
        x_ref, g_ref, w_ref = refs[:3]
        rest = refs[3:]
        xn_ref = rest[-1]
        rest = rest[:-1]

        @pl.when(pl.program_id(1) == 0)
        def _():
            x = x_ref[...]
            ms = jnp.mean(x * x, axis=-1, keepdims=True)
            xn_ref[...] = (x * lax.rsqrt(ms + NORM_EPS) * g_ref[...]).astype(xn_ref.dtype)

        lhs = xn_ref[...]
    else:
        x_ref, w_ref = refs[:2]
        rest = refs[2:]
        lhs = x_ref[...]
    extra = rest[:n_extra]
    outs = rest[n_extra:]
    product = lambda: jnp.dot(lhs, w_ref[...], preferred_element_type=F32)
    epilogue(pl.program_id(1), product, extra, outs)


def _matmul(x, w, *, name, gain=None, extras=(), extra_specs=(), epilogue, out_shapes, out_specs, tm, tn):
    m, k = x.shape
    n = w.shape[1]
    has_norm = gain is not None
    in_specs = [pl.BlockSpec((tm, k), lambda i, j: (i, 0))]
    args = [x]
    if has_norm:
        in_specs.append(pl.BlockSpec((1, k), lambda i, j: (0, 0)))
        args.append(gain.reshape(1, k).astype(F32))
    in_specs.append(pl.BlockSpec((k, tn), lambda i, j: (0, j)))
    args.append(w)
    in_specs += list(extra_specs)
    args += list(extras)
    scratch = [pltpu.VMEM((tm, k), BF16)] if has_norm else []
    return pl.pallas_call(
        functools.partial(_mm_kernel, has_norm=has_norm, n_extra=len(extras), epilogue=epilogue),
        grid=(m // tm, n // tn),
        in_specs=in_specs,
        out_specs=out_specs,
        out_shape=out_shapes,
        scratch_shapes=scratch,
        compiler_params=_params("parallel", "arbitrary"),
        name=name,
    )(*args)


def _ret_in_epilogue(j, product, extra, outs, *, tn, qk_w, v_w):
    cos_ref, sin_ref = extra
    (o_ref,) = outs
    n_qk = (2 * qk_w) // tn
    n_v = v_w // tn

    @pl.when(j < n_qk)
    def _():
        acc = product()
        cos = cos_ref[...]
        sin = sin_ref[...]
        scale = jnp.where(j >= qk_w // tn, 0.0625, 1.0).astype(F32)
        for h in range(tn // 256):
            x1 = acc[:, h * 256:h * 256 + 128]
            x2 = acc[:, h * 256 + 128:(h + 1) * 256]
            o_ref[:, h * 256:h * 256 + 128] = ((x1 * cos - x2 * sin) * scale).astype(o_ref.dtype)
            o_ref[:, h * 256 + 128:(h + 1) * 256] = ((x2 * cos + x1 * sin) * scale).astype(o_ref.dtype)

    @pl.when(jnp.logical_and(j >= n_qk, j < n_qk + n_v))
    def _():
        o_ref[...] = product().astype(o_ref.dtype)

    @pl.when(j >= n_qk + n_v)
    def _():
        acc = product()
        o_ref[...] = (acc * jax.nn.sigmoid(acc)).astype(o_ref.dtype)


def _ret_in_proj(x, gain, w, cos, sin, *, qk_w, v_w, seq, out_dtype):
    m = x.shape[0]
    n = w.shape[1]
    tm = _tile(min(m, seq), 1024)
    tn = _tile(qk_w, 1024)
    nseq = seq // tm
    tab = pl.BlockSpec((tm, LANES), lambda i, j: (i % nseq, 0))
    (out,) = _matmul(
        x, w, name="ret_in_proj", gain=gain, extras=(cos, sin), extra_specs=(tab, tab),
        epilogue=functools.partial(_ret_in_epilogue, tn=tn, qk_w=qk_w, v_w=v_w),
        out_shapes=[jax.ShapeDtypeStruct((m, n), out_dtype)],
        out_specs=[pl.BlockSpec((tm, tn), lambda i, j: (i, j))],
        tm=tm, tn=tn)
    return out


def _ret_log_decay(heads):
    return jnp.log1p(-jnp.exp2(-5.0 - jnp.arange(heads, dtype=F32)))


def _group_norm_gate(o, gw, gb, gate):
    mu = jnp.mean(o, axis=-1, keepdims=True)
    d = o - mu
    var = jnp.mean(d * d, axis=-1, keepdims=True)
    return (d * lax.rsqrt(var + GN_EPS) * gw + gb) * gate


def _ret_chunk_kernel(h_ref, dec_ref, rs_ref, we_ref, gc_ref, gw_ref, gb_ref, y_ref, st_ref, r_ref,
                      *, heads, dk, dv):
    c = pl.program_id(1)
    qk_w = heads * dk
    v_w = heads * dv

    @pl.when(c == 0)
    def _():
        r_ref[...] = jnp.zeros_like(r_ref)

    for h in range(heads):
        q = h_ref[:, h * dk:(h + 1) * dk]
        k = h_ref[:, qk_w + h * dk:qk_w + (h + 1) * dk]
        v = h_ref[:, 2 * qk_w + h * dv:2 * qk_w + (h + 1) * dv]
        g = h_ref[:, 2 * qk_w + v_w + h * dv:2 * qk_w + v_w + (h + 1) * dv]
        r = r_ref[h]
        s = (_nt_dot(q, k) * dec_ref[h]).astype(BF16)
        rs = jnp.concatenate([rs_ref[h]] * (dv // LANES), axis=1)
        o = jnp.dot(s, v, preferred_element_type=F32)
        o = o + jnp.dot(q, r.astype(BF16), preferred_element_type=F32) * rs
        we = jnp.concatenate([we_ref[h]] * (dk // LANES), axis=1)
        kw = (k.astype(F32) * we).astype(BF16)
        gc = jnp.concatenate([gc_ref[h, 0:1, :]] * (dv // LANES), axis=1)
        r_ref[h] = r * gc + _tn_dot(kw, v)
        gw = gw_ref[:, h * dv:(h + 1) * dv]
        gb = gb_ref[:, h * dv:(h + 1) * dv]
        y_ref[:, h * dv:(h + 1) * dv] = _group_norm_gate(o, gw, gb, g.astype(F32)).astype(y_ref.dtype)

    @pl.when(c == pl.num_programs(1) - 1)
    def _():
        st_ref[0] = r_ref[...]


def _ret_prompt(hq, gn_w, gn_b, *, batch, seq, heads, dk, dv):
    chunk = RET_CHUNK
    nc = seq // chunk
    qk_w = heads * dk
    v_w = heads * dv
    log_g = _ret_log_decay(heads)
    n = jnp.arange(chunk, dtype=F32)
    diff = n[:, None] - n[None, :]
    causal = diff >= 0
    decay = jnp.where(causal[None], jnp.exp(jnp.where(causal, diff, 0.0)[None] * log_g[:, None, None]), 0.0)
    ones = jnp.ones((1, 1, LANES), F32)
    rowscale = jnp.exp((n + 1.0)[None, :] * log_g[:, None])[:, :, None] * ones
    w_end = jnp.exp((chunk - 1.0 - n)[None, :] * log_g[:, None])[:, :, None] * ones
    g_chunk = jnp.exp(chunk * log_g)[:, None, None] * jnp.ones((1, 8, LANES), F32)

    tab = lambda shape: pl.BlockSpec(shape, lambda b, c: (0, 0, 0))
    vec = pl.BlockSpec((1, v_w), lambda b, c: (0, 0))
    y, state = pl.pallas_call(
        functools.partial(_ret_chunk_kernel, heads=heads, dk=dk, dv=dv),
        grid=(batch, nc),
        in_specs=[
            pl.BlockSpec((chunk, hq.shape[1]), lambda b, c: (b * nc + c, 0)),
            tab((heads, chunk, chunk)), tab((heads, chunk, LANES)), tab((heads, chunk, LANES)),
            tab((heads, 8, LANES)), vec, vec,
        ],
        out_specs=[
            pl.BlockSpec((chunk, v_w), lambda b, c: (b * nc + c, 0)),
            pl.BlockSpec((1, heads, dk, dv), lambda b, c: (b, 0, 0, 0)),
        ],
        out_shape=[
            jax.ShapeDtypeStruct((batch * seq, v_w), BF16),
            jax.ShapeDtypeStruct((batch, heads, dk, dv), F32),
        ],
        scratch_shapes=[pltpu.VMEM((heads, dk, dv), F32)],
        compiler_params=_params("parallel", "arbitrary"),
        name="ret_chunk_scan",
    )(hq, decay, rowscale, w_end, g_chunk,
      gn_w.reshape(1, v_w).astype(F32), gn_b.reshape(1, v_w).astype(F32))
    return y, state


def _column(row):
    n = row.shape[1]
    eye = lax.broadcasted_iota(jnp.int32, (n, n), 0) == lax.broadcasted_iota(jnp.int32, (n, n), 1)
    return jnp.sum(jnp.where(eye, row, 0.0), axis=1, keepdims=True)


def _ret_sample_heads(gam_ref, head0, q_ref, k_ref, v_ref, g_ref, st_ref, gw_ref, gb_ref, y_ref, ns_ref,
                      *, hb, dk, dv):
    for h in range(hb):
        gam = gam_ref[head0 + h]
        q = q_ref[0, :, h * dk:(h + 1) * dk]
        k = k_ref[0, :, h * dk:(h + 1) * dk]
        v = v_ref[0, :, h * dv:(h + 1) * dv]
        g = g_ref[0, :, h * dv:(h + 1) * dv]
        r = st_ref[0, 0, h]
        s = jnp.sum(q * k, axis=-1, keepdims=True)
        o = s * v + jnp.sum(r * _column(q), axis=0, keepdims=True) * gam
        ns_ref[0, 0, h] = r * gam + _column(k) * v
        gw = gw_ref[:, h * dv:(h + 1) * dv]
        gb = gb_ref[:, h * dv:(h + 1) * dv]
        y_ref[0, :, h * dv:(h + 1) * dv] = _group_norm_gate(o, gw, gb, g).astype(y_ref.dtype)


def _ret_sample_io(unit, hq, state, gn_w, gn_b, *, layer, heads, hb, dk, dv):
    batch, n = hq.shape
    qk_w = heads * dk
    v_w = heads * dv

    def col(width, base):
        return pl.BlockSpec((1, 1, width), lambda *idx: (unit(*idx)[0], 0, base + unit(*idx)[1]))

    def st(lead):
        return pl.BlockSpec((1, 1, hb, dk, dv), lambda *idx: (lead, unit(*idx)[0], unit(*idx)[1], 0, 0))

    vec = pl.BlockSpec((1, hb * dv), lambda *idx: (0, unit(*idx)[1]))
    in_specs = [pl.BlockSpec(memory_space=pltpu.SMEM),
                col(hb * dk, 0), col(hb * dk, heads // hb),
                col(hb * dv, (2 * qk_w) // (hb * dv)), col(hb * dv, (2 * qk_w + v_w) // (hb * dv)),
                st(layer), vec, vec]
    out_specs = [pl.BlockSpec((1, 1, hb * dv), lambda *idx: (unit(*idx)[0], 0, unit(*idx)[1])), st(0)]
    hq3 = hq.reshape(batch, 1, n)
    args = [jnp.exp(1.0 * _ret_log_decay(heads)), hq3, hq3, hq3, hq3, state,
            gn_w.reshape(1, v_w).astype(F32), gn_b.reshape(1, v_w).astype(F32)]
    out_shapes = [jax.ShapeDtypeStruct((batch, 1, v_w), BF16),
                  jax.ShapeDtypeStruct((1,) + state.shape[1:], F32)]
    return in_specs, args, out_specs, out_shapes


def _ret_sample_kernel(gam_ref, *refs, heads, dk, dv):
    _ret_sample_heads(gam_ref, 0, *refs, hb=heads, dk=dk, dv=dv)


def _ret_sample(hq, state, gn_w, gn_b, *, layer, heads, dk, dv):
    batch = hq.shape[0]
    in_specs, args, out_specs, out_shapes = _ret_sample_io(
        lambda b: (b, 0), hq, state, gn_w, gn_b, layer=layer, heads=heads, hb=heads, dk=dk, dv=dv)
    y, new_state = pl.pallas_call(
        functools.partial(_ret_sample_kernel, heads=heads, dk=dk, dv=dv),
        grid=(batch,),
        in_specs=in_specs,
        out_specs=out_specs,
        out_shape=out_shapes,
        compiler_params=_params("parallel"),
        name="ret_sample_step",
    )(*args)
    return y.reshape(batch, heads * dv), new_state


def _residual_epilogue(j, product, extra, outs):
    (res_ref,) = extra
    (o_ref,) = outs
    o_ref[...] = res_ref[...] + product()


def _proj_residual(a, w, res):
    m = a.shape[0]
    n = w.shape[1]
    tm = _tile(m, 512)
    tn = _tile(n, 1024)
    blk = pl.BlockSpec((tm, tn), lambda i, j: (i, j))
    (out,) = _matmul(a, w, name="proj_residual", extras=(res,), extra_specs=(blk,), epilogue=_residual_epilogue,
                     out_shapes=[jax.ShapeDtypeStruct((m, n), F32)], out_specs=[blk], tm=tm, tn=tn)
    return out


RIDER_ARITY = {"ret": (8, 2, 0), "paged": (7, 1, 6)}


def _mlp_kernel(x_ref, g_ref, wu_ref, wd_ref, *rest, final_norm, rider):
    rest = list(rest)
    n_in, n_out, n_scr = RIDER_ARITY[rider[0]] if rider else (0, 0, 0)
    fg_ref = rest.pop(0) if final_norm else None
    ride_in = [rest.pop(0) for _ in range(n_in)]
    o_ref = rest.pop(0)
    ride_out = [rest.pop(0) for _ in range(n_out)]
    xn_ref = rest.pop(0)
    ride_scr = rest
    assert len(ride_scr) == n_scr
    f = pl.program_id(1)
    step = pl.program_id(0) * pl.num_programs(1) + f
    n_steps = pl.num_programs(0) * pl.num_programs(1)

    @pl.when(f == 0)
    def _():
        x = x_ref[...]
        ms = jnp.mean(x * x, axis=-1, keepdims=True)
        xn_ref[...] = (x * lax.rsqrt(ms + NORM_EPS) * g_ref[...]).astype(xn_ref.dtype)
        o_ref[...] = x

    if rider and rider[0] == "paged":
        _, kw, n_groups = rider
        pt_ref, ql_ref, qr_ref, cn_ref, kn_ref, ckv_hbm, kr_hbm = ride_in
        ckv_buf, kr_buf, sem, m_ref, l_ref, acc_ref = ride_scr
        _paged_prologue(step, n_steps, n_groups, pt_ref, ckv_hbm, kr_hbm, ckv_buf, kr_buf, sem, m_ref, l_ref, acc_ref,
                        layer=kw["layer"], sps=kw["sps"], pps=kw["pps"])

    h = jnp.dot(xn_ref[...], wu_ref[0], preferred_element_type=F32)
    if rider and rider[0] == "ret":
        _, hb, dk, dv, n_split = rider
        _ret_sample_heads(ride_in[0], lax.rem(step, n_split) * hb, *ride_in[1:], *ride_out, hb=hb, dk=dk, dv=dv)
    if rider and rider[0] == "paged":
        _paged_compute(step, ql_ref, qr_ref, ckv_buf, kr_buf, m_ref, l_ref, acc_ref,
                       sps=kw["sps"], pps=kw["pps"], rope_dim=kw["rope_dim"])
    h = jnp.square(jnp.maximum(h, 0.0)).astype(BF16)
    o_ref[...] += jnp.dot(h, wd_ref[0], preferred_element_type=F32)

    if rider and rider[0] == "paged":
        _paged_epilogue(step, n_groups, ql_ref, qr_ref, cn_ref, kn_ref, ride_out[0], m_ref, l_ref, acc_ref,
                        sps=kw["sps"], rope_dim=kw["rope_dim"])

    if final_norm:
        @pl.when(f == pl.num_programs(1) - 1)
        def _():
            y = o_ref[...]
            ms = jnp.mean(y * y, axis=-1, keepdims=True)
            o_ref[...] = y * lax.rsqrt(ms + NORM_EPS) * fg_ref[...]


def _mlp_residual(x, gain, w_up, w_down, layer, final_gain=None, ret_sample=None, paged=None):
    m, d = x.shape
    ff = w_up.shape[2]
    tm = _tile(m, 512)
    tf = _tile(ff, 1024)
    rider = None
    if paged is not None:
        sps, pps, n_seq_groups, n_groups = _paged_plan(paged["q_lat"], paged["page_table"])
        per_row, rem = divmod(n_seq_groups * n_groups, m // tm)
        if rem == 0 and per_row >= 1 and ff % per_row == 0 and (ff // per_row) % LANES == 0:
            tf = ff // per_row
            rider = ("paged", dict(layer=paged["layer"], sps=sps, pps=pps, rope_dim=paged["cache_kr_t"].shape[2]),
                     n_groups)
    grid = (m // tm, ff // tf)
    final_norm = final_gain is not None
    vec = pl.BlockSpec((1, d), lambda i, f: (0, 0))
    in_specs = [
        pl.BlockSpec((tm, d), lambda i, f: (i, 0)),
        vec,
        pl.BlockSpec((1, d, tf), lambda i, f: (layer, 0, f)),
        pl.BlockSpec((1, tf, d), lambda i, f: (layer, f, 0)),
    ]
    args = [x, gain.reshape(1, d).astype(F32), w_up, w_down]
    if final_norm:
        in_specs.append(vec)
        args.append(final_gain.reshape(1, d).astype(F32))
    out_specs = [pl.BlockSpec((tm, d), lambda i, f: (i, 0))]
    out_shapes = [jax.ShapeDtypeStruct((m, d), F32)]
    scratch = [pltpu.VMEM((tm, d), BF16)]
    if rider is not None:
        p = paged
        r_in, r_args, r_out, r_shapes, r_scratch = _paged_io(
            lambda i, f: (i * grid[1] + f) // n_groups, p["q_lat"], p["qr"], p["ckv_new"], p["kr_new"],
            p["cache_ckv"], p["cache_kr_t"], p["page_table"])
        in_specs += r_in
        args += r_args
        out_specs += r_out
        out_shapes += r_shapes
        scratch += r_scratch
    if ret_sample is not None:
        r = ret_sample
        batch = r["hq"].shape[0]
        n_split, rem = divmod(grid[0] * grid[1], batch)
        if rem == 0 and n_split >= 1 and r["heads"] % n_split == 0:
            hb = r["heads"] // n_split
            unit = lambda i, f: ((i * grid[1] + f) // n_split, (i * grid[1] + f) % n_split)
            r_in, r_args, r_out, r_shapes = _ret_sample_io(
                unit, r["hq"], r["state"], r["gn_w"], r["gn_b"],
                layer=r["layer"], heads=r["heads"], hb=hb, dk=r["dk"], dv=r["dv"])
            in_specs += r_in
            args += r_args
            out_specs += r_out
            out_shapes += r_shapes
            rider = ("ret", hb, r["dk"], r["dv"], n_split)
    outs = pl.pallas_call(
        functools.partial(_mlp_kernel, final_norm=final_norm, rider=rider),
        grid=grid,
        in_specs=in_specs,
        out_specs=out_specs,
        out_shape=out_shapes,
        scratch_shapes=scratch,
        compiler_params=_params("arbitrary" if rider else "parallel", "arbitrary"),
        name="mlp_residual_" + rider[0] if rider else "mlp_residual",
    )(*args)
    if paged is not None:
        return outs[0], (outs[1] if rider else None)
    if ret_sample is None:
        return outs[0]
    if rider is None:
        return outs[0], None, None
    return outs[0], outs[1].reshape(outs[1].shape[0], -1), outs[2]


def _mla_in_epilogue(j, product, extra, outs, *, q_lora, kv_lora):
    qg_ref, kg_ref, cos_ref, sa_ref, sb_ref = extra
    cq_ref, ckv_ref, ckvb_ref, kr_ref, krb_ref = outs
    acc = product()

    def norm(x, g):
        return x * lax.rsqrt(jnp.mean(x * x, axis=-1, keepdims=True) + NORM_EPS) * g

    cq_ref[...] = norm(acc[:, :q_lora], qg_ref[...]).astype(cq_ref.dtype)
    ckv = norm(acc[:, q_lora:q_lora + kv_lora], kg_ref[...])
    ckv_ref[...] = ckv
    ckvb_ref[...] = ckv.astype(ckvb_ref.dtype)
    kr = _rope64(acc[:, q_lora + kv_lora:], cos_ref[...], sa_ref[...], sb_ref[...])
    kr_ref[...] = kr
    krb_ref[...] = kr.astype(krb_ref.dtype)


def _mla_in_proj(x, gain, w, q_gain, kv_gain, tabs, *, q_lora, kv_lora, seq):
    m = x.shape[0]
    n = w.shape[1]
    tm = _tile(min(m, seq), 512)
    nseq = seq // tm
    tab = pl.BlockSpec((tm, LANES), lambda i, j: (i % nseq, 0))
    vec = lambda width: pl.BlockSpec((1, width), lambda i, j: (0, 0))
    blk = lambda width: pl.BlockSpec((tm, width), lambda i, j: (i, 0))
    return _matmul(
        x, w, name="mla_in_proj", gain=gain,
        extras=(q_gain.reshape(1, q_lora).astype(F32), kv_gain.reshape(1, kv_lora).astype(F32)) + tuple(tabs),
        extra_specs=(vec(q_lora), vec(kv_lora), tab, tab, tab),
        epilogue=functools.partial(_mla_in_epilogue, q_lora=q_lora, kv_lora=kv_lora),
        out_shapes=[jax.ShapeDtypeStruct((m, q_lora), BF16),
                    jax.ShapeDtypeStruct((m, kv_lora), F32),
                    jax.ShapeDtypeStruct((m, kv_lora), BF16),
                    jax.ShapeDtypeStruct((m, LANES), F32),
                    jax.ShapeDtypeStruct((m, LANES), BF16)],
        out_specs=[blk(q_lora), blk(kv_lora), blk(kv_lora), blk(LANES), blk(LANES)],
        tm=tm, tn=n)


def _mla_q_epilogue(j, product, extra, outs, *, n_nope, scale):
    cos_ref, sa_ref, sb_ref = extra
    (o_ref,) = outs

    @pl.when(j < n_nope)
    def _():
        o_ref[...] = (product() * scale).astype(o_ref.dtype)

    @pl.when(j >= n_nope)
    def _():
        acc = product()
        cos = cos_ref[...]
        sa = sa_ref[...]
        sb = sb_ref[...]
        for h in range(acc.shape[1] // LANES):
            x = acc[:, h * LANES:(h + 1) * LANES]
            o_ref[:, h * LANES:(h + 1) * LANES] = (_rope64(x, cos, sa, sb) * scale).astype(o_ref.dtype)


def _mla_q_proj(cq, w, tabs, *, nope_w, scale, seq, out_dtype):
    m = cq.shape[0]
    n = w.shape[1]
    tm = _tile(min(m, seq), 1024)
    tn = _tile(nope_w, 1024)
    nseq = seq // tm
    tab = pl.BlockSpec((tm, LANES), lambda i, j: (i % nseq, 0))
    (out,) = _matmul(
        cq, w, name="mla_q_proj", extras=tuple(tabs), extra_specs=(tab, tab, tab),
        epilogue=functools.partial(_mla_q_epilogue, n_nope=nope_w // tn, scale=scale),
        out_shapes=[jax.ShapeDtypeStruct((m, n), out_dtype)],
        out_specs=[pl.BlockSpec((tm, tn), lambda i, j: (i, j))],
        tm=tm, tn=tn)
    return out


def _cast_epilogue(j, product, extra, outs):
    (o_ref,) = outs
    o_ref[...] = product().astype(o_ref.dtype)


def _plain_matmul(a, w, out_dtype):
    m = a.shape[0]
    n = w.shape[1]
    tm = _tile(m, 1024)
    tn = _tile(n, 1024)
    (out,) = _matmul(a, w, name="mla_k_expand", epilogue=_cast_epilogue,
                     out_shapes=[jax.ShapeDtypeStruct((m, n), out_dtype)],
                     out_specs=[pl.BlockSpec((tm, tn), lambda i, j: (i, j))], tm=tm, tn=tn)
    return out


FLASH_TILE = 512
FLASH_HEADS_PER_STEP = 4


def _vt_kernel(x_ref, wt_ref, o_ref):
    o_ref[0] = _nt_dot(wt_ref[...], x_ref[...]).astype(o_ref.dtype)


def _mla_v_expand_t(ckvb, w_uv_t, tq):
    m, c = ckvb.shape
    n = w_uv_t.shape[0]
    tn = _tile(n, 1024)
    return pl.pallas_call(
        _vt_kernel,
        grid=(m // tq, n // tn),
        in_specs=[pl.BlockSpec((tq, c), lambda i, j: (i, 0)), pl.BlockSpec((tn, c), lambda i, j: (j, 0))],
        out_specs=pl.BlockSpec((1, tn, tq), lambda i, j: (i, j, 0)),
        out_shape=jax.ShapeDtypeStruct((m // tq, n, tq), BF16),
        compiler_params=_params("parallel", "arbitrary"),
        name="mla_v_expand_t",
    )(ckvb, w_uv_t)


def _flash_kernel(qn_ref, qr_ref, kn_ref, kr_ref, vt_ref, o_ref, *, tq, hps):
    qi = pl.program_id(2)
    dv = vt_ref.shape[1] // hps
    qs = [jnp.concatenate([qn_ref[:, h * LANES:(h + 1) * LANES], qr_ref[:, h * LANES:(h + 1) * LANES]], axis=1)
          for h in range(hps)]

    def scores(h, start):
        k = jnp.concatenate([kn_ref[pl.ds(start, tq), h * LANES:(h + 1) * LANES], kr_ref[pl.ds(start, tq), :]],
                            axis=1)
        return _nt_dot(k, qs[h])

    ones = jnp.ones((BF16_SUBLANES, tq), BF16)

    def probs(carry, s):
        m, _ = carry
        m_new = jnp.maximum(m, jnp.max(s, axis=0, keepdims=True))
        return m_new, jnp.exp2(m - m_new), jnp.exp2(s - m_new).astype(BF16)

    def update_all(carries, s_all, kb):
        st = [probs(carries[h], s_all[h]) for h in range(hps)]
        out = []
        for h in range(hps):
            m_new, corr, pb = st[h]
            vt = jnp.concatenate([vt_ref[kb, h * dv:(h + 1) * dv, :], ones], axis=0)
            out.append((m_new, carries[h][1] * corr + jnp.dot(vt, pb, preferred_element_type=F32)))
        return tuple(out)

    def body(kb, carries):
        start = pl.multiple_of(kb * tq, tq)
        return update_all(carries, [scores(h, start) for h in range(hps)], kb)

    init = tuple((jnp.full((1, tq), NEG, F32), jnp.zeros((dv + BF16_SUBLANES, tq), F32)) for _ in range(hps))
    carries = lax.fori_loop(0, qi, body, init)
    start = pl.multiple_of(qi * tq, tq)
    causal = lax.broadcasted_iota(jnp.int32, (tq, tq), 0) <= lax.broadcasted_iota(jnp.int32, (tq, tq), 1)
    final = update_all(carries, [jnp.where(causal, scores(h, start), NEG) for h in range(hps)], qi)
    for h in range(hps):
        acc = final[h][1]
        o_ref[:, h * dv:(h + 1) * dv] = (acc[:dv] / acc[dv:dv + 1]).T.astype(o_ref.dtype)


def _mla_prompt_attention(q, kn, krb, vt, *, batch, seq, heads):
    m = q.shape[0]
    tq = vt.shape[2]
    nq = seq // tq
    hps = FLASH_HEADS_PER_STEP if heads % FLASH_HEADS_PER_STEP == 0 else 1
    w = hps * LANES
    ng = heads // hps
    qblk = lambda off: pl.BlockSpec((tq, w), lambda b, h, i: (b * nq + i, off + h))
    return pl.pallas_call(
        functools.partial(_flash_kernel, tq=tq, hps=hps),
        grid=(batch, ng, nq),
        in_specs=[qblk(0), qblk(ng),
                  pl.BlockSpec((seq, w), lambda b, h, i: (b, h)),
                  pl.BlockSpec((seq, LANES), lambda b, h, i: (b, 0)),
                  pl.BlockSpec((nq, w, tq), lambda b, h, i: (b, h, 0))],
        out_specs=pl.BlockSpec((tq, w), lambda b, h, i: (b * nq + i, h)),
        out_shape=jax.ShapeDtypeStruct((m, heads * LANES), BF16),
        compiler_params=_params("parallel", "parallel", "arbitrary"),
        name="mla_prompt_flash",
    )(q, q, kn, krb, vt)


def _head_nt_kernel(a_ref, w_ref, o_ref):
    o_ref[0] = _nt_dot(a_ref[...], w_ref[...]).astype(o_ref.dtype)


def _absorb_uk(qn, w_uk2, *, heads):
    b = qn.shape[0]
    c = w_uk2.shape[0]
    return pl.pallas_call(
        _head_nt_kernel,
        grid=(heads,),
        in_specs=[pl.BlockSpec((b, LANES), lambda h: (0, h)), pl.BlockSpec((c, LANES), lambda h: (0, h))],
        out_specs=pl.BlockSpec((1, b, c), lambda h: (h, 0, 0)),
        out_shape=jax.ShapeDtypeStruct((heads, b, c), F32),
        compiler_params=_params("parallel"),
        name="mla_absorb_uk",
    )(qn, w_uk2)


def _head_nn_kernel(a_ref, w_ref, o_ref):
    o_ref[...] = jnp.dot(a_ref[0], w_ref[...], preferred_element_type=F32).astype(o_ref.dtype)


def _expand_uv(o_lat, w_uv2, *, heads):
    b = o_lat.shape[1]
    c = o_lat.shape[2]
    return pl.pallas_call(
        _head_nn_kernel,
        grid=(heads,),
        in_specs=[pl.BlockSpec((1, b, c), lambda h: (h, 0, 0)), pl.BlockSpec((c, LANES), lambda h: (0, h))],
        out_specs=pl.BlockSpec((b, LANES), lambda h: (0, h)),
        out_shape=jax.ShapeDtypeStruct((b, heads * LANES), BF16),
        compiler_params=_params("parallel"),
        name="mla_expand_uv",
    )(o_lat, w_uv2)


def _paged_prologue(step, n_steps, n_groups, pt_ref, ckv_hbm, kr_hbm, ckv_buf, kr_buf, sem, m_ref, l_ref, acc_ref,
                    *, layer, sps, pps):
    g = lax.rem(step, n_groups)
    slot = lax.rem(step, 2)

    def gather(t, buf_slot):
        row0 = (t // n_groups) * sps
        col0 = lax.rem(t, n_groups) * pps
        out = []
        for s in range(sps):
            for k in range(pps):
                pid = pt_ref[row0 + s, col0 + k]
                j = s * pps + k
                out.append(pltpu.make_async_copy(ckv_hbm.at[layer, pid], ckv_buf.at[buf_slot, j], sem.at[0, buf_slot]))
                out.append(pltpu.make_async_copy(kr_hbm.at[layer, pid], kr_buf.at[buf_slot, j], sem.at[1, buf_slot]))
        return out

    @pl.when(step == 0)
    def _():
        for c in gather(step, slot):
            c.start()

    @pl.when(step + 1 < n_steps)
    def _():
        for c in gather(step + 1, 1 - slot):
            c.start()

    for c in gather(step, slot):
        c.wait()

    @pl.when(g == 0)
    def _():
        m_ref[...] = jnp.full_like(m_ref, NEG)
        l_ref[...] = jnp.zeros_like(l_ref)
        acc_ref[...] = jnp.zeros_like(acc_ref)


def _paged_compute(step, ql_ref, qr_ref, ckv_buf, kr_buf, m_ref, l_ref, acc_ref, *, sps, pps, rope_dim):
    slot = lax.rem(step, 2)
    ql = [ql_ref[s] for s in range(sps)]
    qr = [qr_ref[s][:, :rope_dim] for s in range(sps)]
    pages = [[ckv_buf[slot, s * pps + i].astype(BF16) for i in range(pps)] for s in range(sps)]
    page = pages[0][0].shape[0]
    scores = []
    for s in range(sps):
        qlb = ql[s].astype(BF16)
        qrb = qr[s].astype(BF16)
        scores.append(jnp.concatenate(
            [_nt_dot(qlb, pages[s][i])
             + jnp.dot(qrb, kr_buf[slot, s * pps + i].astype(BF16), preferred_element_type=F32)
             for i in range(pps)], axis=1))
    stats = []
    for s in range(sps):
        m = m_ref[s]
        m_new = jnp.maximum(m, jnp.max(scores[s], axis=-1, keepdims=True))
        corr = jnp.exp(m - m_new)
        p = jnp.exp(scores[s] - m_new)
        l_new = l_ref[s] * corr + jnp.sum(p, axis=-1, keepdims=True)
        stats.append((m_new, corr, l_new, p.astype(BF16)))
    for s in range(sps):
        m_new, corr, l_new, pb = stats[s]
        pv = jnp.dot(pb[:, :page], pages[s][0], preferred_element_type=F32)
        for i in range(1, pps):
            pv = pv + jnp.dot(pb[:, i * page:(i + 1) * page], pages[s][i], preferred_element_type=F32)
        acc_new = acc_ref[s] * corr + pv
        m_ref[s] = m_new
        l_ref[s] = l_new
        acc_ref[s] = acc_new


def _paged_epilogue(step, n_groups, ql_ref, qr_ref, cn_ref, kn_ref, o_ref, m_ref, l_ref, acc_ref, *, sps, rope_dim):
    @pl.when(lax.rem(step, n_groups) == n_groups - 1)
    def _():
        for s in range(sps):
            ql = ql_ref[s]
            m_new = m_ref[s]
            cn = cn_ref[s]
            s_new = (jnp.sum(ql * cn, axis=-1, keepdims=True)
                     + jnp.sum(qr_ref[s][:, :rope_dim] * kn_ref[s][:, :rope_dim], axis=-1, keepdims=True))
            m_fin = jnp.maximum(m_new, s_new)
            c_fin = jnp.exp(m_new - m_fin)
            p_new = jnp.exp(s_new - m_fin)
            l_fin = l_ref[s] * c_fin + p_new
            o_ref[s] = (acc_ref[s] * c_fin + p_new * cn) / l_fin


def _paged_kernel(pt_ref, ql_ref, qr_ref, cn_ref, kn_ref, ckv_hbm, kr_hbm, o_ref,
                  ckv_buf, kr_buf, sem, m_ref, l_ref, acc_ref, *, layer, sps, pps, rope_dim):
    n_groups = pl.num_programs(1)
    step = pl.program_id(0) * n_groups + pl.program_id(1)
    n_steps = pl.num_programs(0) * n_groups
    _paged_prologue(step, n_steps, n_groups, pt_ref, ckv_hbm, kr_hbm, ckv_buf, kr_buf, sem, m_ref, l_ref, acc_ref,
                    layer=layer, sps=sps, pps=pps)
    _paged_compute(step, ql_ref, qr_ref, ckv_buf, kr_buf, m_ref, l_ref, acc_ref, sps=sps, pps=pps, rope_dim=rope_dim)
    _paged_epilogue(step, n_groups, ql_ref, qr_ref, cn_ref, kn_ref, o_ref, m_ref, l_ref, acc_ref,
                    sps=sps, rope_dim=rope_dim)


PAGES_PER_STEP = 16
SEQS_PER_STEP = 2


def _paged_plan(q_lat, page_table):
    b = q_lat.shape[0]
    n_pages = page_table.shape[1]
    pps = _tile(n_pages, PAGES_PER_STEP)
    sps = SEQS_PER_STEP if b % SEQS_PER_STEP == 0 else 1
    return sps, pps, b // sps, n_pages // pps


def _paged_io(unit, q_lat, qr, ckv_new, kr_new, cache_ckv, cache_kr_t, page_table):
    b, heads, c = q_lat.shape
    page = cache_ckv.shape[2]
    rope_dim = cache_kr_t.shape[2]
    sps, pps, _, _ = _paged_plan(q_lat, page_table)
    per_b = lambda width: pl.BlockSpec((sps, heads, width), lambda *idx: (unit(*idx), 0, 0))
    new_b = lambda width: pl.BlockSpec((sps, 1, width), lambda *idx: (unit(*idx), 0, 0))
    hbm = pl.BlockSpec(memory_space=pl.ANY)
    in_specs = [pl.BlockSpec(memory_space=pltpu.SMEM), per_b(c), per_b(LANES), new_b(c), new_b(LANES), hbm, hbm]
    args = [page_table, q_lat, qr, ckv_new.reshape(b, 1, c), kr_new.reshape(b, 1, LANES), cache_ckv, cache_kr_t]
    out_specs = [per_b(c)]
    out_shapes = [jax.ShapeDtypeStruct((b, heads, c), F32)]
    n_slots = sps * pps
    scratch = [pltpu.VMEM((2, n_slots, page, c), cache_ckv.dtype),
               pltpu.VMEM((2, n_slots, rope_dim, page), cache_kr_t.dtype),
               pltpu.SemaphoreType.DMA((2, 2)),
               pltpu.VMEM((sps, heads, 1), F32), pltpu.VMEM((sps, heads, 1), F32),
               pltpu.VMEM((sps, heads, c), F32)]
    return in_specs, args, out_specs, out_shapes, scratch


def _mla_paged_attention(q_lat, qr, ckv_new, kr_new, cache_ckv, cache_kr_t, page_table, *, layer):
    sps, pps, n_seq_groups, n_groups = _paged_plan(q_lat, page_table)
    in_specs, args, out_specs, out_shapes, scratch = _paged_io(
        lambda i, g: i, q_lat, qr, ckv_new, kr_new, cache_ckv, cache_kr_t, page_table)
    (o_lat,) = pl.pallas_call(
        functools.partial(_paged_kernel, layer=layer, sps=sps, pps=pps, rope_dim=cache_kr_t.shape[2]),
        grid=(n_seq_groups, n_groups),
        in_specs=in_specs,
        out_specs=out_specs,
        out_shape=out_shapes,
        scratch_shapes=scratch,
        compiler_params=_params("arbitrary", "arbitrary"),
        name="mla_paged_attention",
    )(*args)
    return o_lat


def _concat_layers(parts):
    return parts[0] if len(parts) == 1 else jnp.concatenate(parts, axis=0)


def kernel(x_prompt, x_sample, state_ret, cache_kv_latent, cache_k_rope, page_table, norm_mix, norm_mlp, norm_final, w_ret_in, ret_gn_w, ret_gn_b, w_ret_out, w_mla_in, mla_q_norm, mla_kv_norm, w_mla_uq, w_mla_uk, w_mla_uv, w_mla_out, w_up, w_down):
    bp, seq, d = x_prompt.shape
    bs, ls, _ = x_sample.shape
    assert ls == 1, "the sample group is a single new token per sequence"
    depth = norm_mix.shape[0]
    assert depth >= 1, "the final norm is fused into the last layer's MLP"
    ret_heads, ret_dk, ret_dv = state_ret.shape[2:]
    qk_w = ret_heads * ret_dk
    v_w = ret_heads * ret_dv
    kv_lora, mla_heads, nope = w_mla_uk.shape[1:]
    v_dim = w_mla_uv.shape[3]
    q_lora = mla_q_norm.shape[1]
    rope_dim = cache_k_rope.shape[3]
    page = cache_kv_latent.shape[2]
    assert ret_dk == 2 * LANES and nope == LANES and v_dim == LANES and rope_dim * 2 == LANES
    past = page_table.shape[1] * page
    scale = float((nope + rope_dim) ** -0.5)

    pos_p = jnp.arange(seq, dtype=F32)
    pos_s = jnp.broadcast_to(past + jnp.arange(ls, dtype=F32), (bs,))
    ret_tab_p = _rope_cos_sin(pos_p, ret_dk // 2)
    ret_tab_s = _rope_cos_sin(pos_s, ret_dk // 2)
    mla_tab_p = _rope64_tables(pos_p)
    mla_tab_s = _rope64_tables(pos_s)

    xp = x_prompt.reshape(bp * seq, d)
    xs = x_sample.reshape(bs, d)
    w_up_b = w_up.astype(BF16)
    w_down_b = w_down.astype(BF16)
    ret_p, ret_s, ckv_p, kr_p, ckv_s, kr_s = [], [], [], [], [], []
    for i in range(depth):
        j = i // 2
        ride = None
        if i % 2 == 0:
            w_in = w_ret_in[j].astype(BF16)
            w_out = w_ret_out[j].astype(BF16)
            hq = _ret_in_proj(xp, norm_mix[i], w_in, *ret_tab_p, qk_w=qk_w, v_w=v_w, seq=seq, out_dtype=BF16)
            y, r_p = _ret_prompt(hq, ret_gn_w[j], ret_gn_b[j], batch=bp, seq=seq,
                                 heads=ret_heads, dk=ret_dk, dv=ret_dv)
            xp = _proj_residual(y, w_out, xp)
            hq = _ret_in_proj(xs, norm_mix[i], w_in, *ret_tab_s, qk_w=qk_w, v_w=v_w, seq=bs, out_dtype=F32)
            ride = dict(hq=hq, state=state_ret, gn_w=ret_gn_w[j], gn_b=ret_gn_b[j], layer=j,
                        heads=ret_heads, dk=ret_dk, dv=ret_dv)
            ret_p.append(r_p[None])
        else:
            w_in = jnp.pad(w_mla_in[j], ((0, 0), (0, LANES - rope_dim))).astype(BF16)
            w_uq = w_mla_uq[j].reshape(q_lora, mla_heads, nope + rope_dim)
            w_uq = jnp.concatenate(
                [w_uq[:, :, :nope].reshape(q_lora, mla_heads * nope),
                 jnp.pad(w_uq[:, :, nope:], ((0, 0), (0, 0), (0, LANES - rope_dim))).reshape(q_lora, mla_heads * LANES)],
                axis=1).astype(BF16)
            w_uk2 = w_mla_uk[j].reshape(kv_lora, mla_heads * nope).astype(BF16)
            w_uv2 = w_mla_uv[j].reshape(kv_lora, mla_heads * v_dim).astype(BF16)
            w_out = w_mla_out[j].astype(BF16)
            nope_w = mla_heads * nope

            cq, ckv, ckvb, kr, krb = _mla_in_proj(xp, norm_mix[i], w_in, mla_q_norm[j], mla_kv_norm[j], mla_tab_p,
                                                  q_lora=q_lora, kv_lora=kv_lora, seq=seq)
            q = _mla_q_proj(cq, w_uq, mla_tab_p, nope_w=nope_w, scale=scale * LOG2_E, seq=seq, out_dtype=BF16)
            kn = _plain_matmul(ckvb, w_uk2, BF16)
            vt = _mla_v_expand_t(ckvb, w_uv2.T, _tile(seq, FLASH_TILE))
            o = _mla_prompt_attention(q, kn, krb, vt, batch=bp, seq=seq, heads=mla_heads)
            xp = _proj_residual(o, w_out, xp)
            ckv_p.append(ckv.reshape(1, bp, seq // page, page, kv_lora))
            kr_p.append(kr[:, :rope_dim].reshape(1, bp, seq // page, page, rope_dim))

            cq, ckv, ckvb, kr, krb = _mla_in_proj(xs, norm_mix[i], w_in, mla_q_norm[j], mla_kv_norm[j], mla_tab_s,
                                                  q_lora=q_lora, kv_lora=kv_lora, seq=bs)
            q = _mla_q_proj(cq, w_uq, mla_tab_s, nope_w=nope_w, scale=scale, seq=bs, out_dtype=F32)
            q_lat = _absorb_uk(q[:, :nope_w].astype(BF16), w_uk2, heads=mla_heads)
            ride_paged = dict(q_lat=q_lat.transpose(1, 0, 2), qr=q[:, nope_w:].reshape(bs, mla_heads, LANES),
                              ckv_new=ckv, kr_new=kr, cache_ckv=cache_kv_latent,
                              cache_kr_t=jnp.swapaxes(cache_k_rope, 2, 3), page_table=page_table, layer=j)
            ckv_s.append(ckv.reshape(1, bs, ls, kv_lora))
            kr_s.append(kr[:, :rope_dim].reshape(1, bs, ls, rope_dim))
        final_gain = norm_final if i == depth - 1 else None
        if i % 2 == 1:
            xp, o_lat = _mlp_residual(xp, norm_mlp[i], w_up_b, w_down_b, i, final_gain, paged=ride_paged)
            if o_lat is None:
                p = ride_paged
                o_lat = _mla_paged_attention(p["q_lat"], p["qr"], p["ckv_new"], p["kr_new"], p["cache_ckv"],
                                             p["cache_kr_t"], p["page_table"], layer=j)
            o = _expand_uv(o_lat.transpose(1, 0, 2).astype(BF16), w_uv2, heads=mla_heads)
            xs = _proj_residual(o, w_out, xs)
        elif ride is None:
            xp = _mlp_residual(xp, norm_mlp[i], w_up_b, w_down_b, i, final_gain)
        else:
            xp, y, r_s = _mlp_residual(xp, norm_mlp[i], w_up_b, w_down_b, i, final_gain, ret_sample=ride)
            if y is None:
                y, r_s = _ret_sample(ride["hq"], state_ret, ride["gn_w"], ride["gn_b"], layer=j,
                                     heads=ret_heads, dk=ret_dk, dv=ret_dv)
            xs = _proj_residual(y, w_out, xs)
            ret_s.append(r_s)
        xs = _mlp_residual(xs, norm_mlp[i], w_up_b, w_down_b, i, final_gain)
    y_prompt = xp.reshape(bp, seq, d)
    y_sample = xs.reshape(bs, ls, d)
    return (y_prompt, y_sample) + tuple(_concat_layers(t) for t in (ret_p, ret_s, ckv_p, kr_p, ckv_s, kr_s))
```

```python
import functools

import jax
import jax.numpy as jnp
from jax import lax
from jax.experimental import pallas as pl
from jax.experimental.pallas import tpu as pltpu

ROPE_BASE = 10000.0
NORM_EPS = 1e-6
GN_EPS = 1e-5
NEG = -1e30
RET_CHUNK = 256
LOG2_E = 1.4426950408889634

LANES = 128
BF16_SUBLANES = 16
V7X_VMEM_BYTES = 64 * 1024 * 1024
VMEM_LIMIT = 52 * 1024 * 1024

F32 = jnp.float32
BF16 = jnp.bfloat16


def _params(*sem):
    return pltpu.CompilerParams(dimension_semantics=sem, vmem_limit_bytes=VMEM_LIMIT)


def _tile(n, pref):
    if n <= pref:
        return n
    t = pref
    while n % t:
        t //= 2
    return t


def _nt_dot(a, b):
    return lax.dot_general(a, b, (((1,), (1,)), ((), ())), preferred_element_type=F32)


def _tn_dot(a, b):
    return lax.dot_general(a, b, (((0,), (0,)), ((), ())), preferred_element_type=F32)


def _rope_cos_sin(pos, d2):
    inv = ROPE_BASE ** (-jnp.arange(d2, dtype=F32) / d2)
    ang = pos[:, None] * inv[None, :]
    return jnp.cos(ang), jnp.sin(ang)


def _rope64_tables(pos):
    cos, sin = _rope_cos_sin(pos, 32)
    z = jnp.zeros_like(sin)
    cos_t = jnp.concatenate([cos, cos, cos, cos], axis=-1)
    sa = jnp.concatenate([-sin, z, -sin, z], axis=-1)
    sb = jnp.concatenate([z, sin, z, sin], axis=-1)
    return cos_t, sa, sb


def _rope64(x, cos_t, sa, sb):
    return x * cos_t + pltpu.roll(x, 96, 1) * sa + pltpu.roll(x, 32, 1) * sb


def _mm_kernel(*refs, has_norm, n_extra, epilogue):
    if has_norm:
        x_ref, g_ref, w_ref = refs[:3]
        rest = refs[3:]
        xn_ref = rest[-1]
        rest = rest[:-1]

        @pl.when(pl.program_id(1) == 0)
        def _():
            x = x_ref[...]
            ms = jnp.mean(x * x, axis=-1, keepdims=True)
            xn_ref[...] = (x * lax.rsqrt(ms + NORM_EPS) * g_ref[...]).astype(xn_ref.dtype)

        lhs = xn_ref[...]
    else:
        x_ref, w_ref = refs[:2]
        rest = refs[2:]
        lhs = x_ref[...]
    extra = rest[:n_extra]
    outs = rest[n_extra:]
    product = lambda: jnp.dot(lhs, w_ref[...], preferred_element_type=F32)
    epilogue(pl.program_id(1), product, extra, outs)


def _matmul(x, w, *, name, gain=None, extras=(), extra_specs=(), epilogue, out_shapes, out_specs, tm, tn):
    m, k = x.shape
    n = w.shape[1]
    has_norm = gain is not None
    in_specs = [pl.BlockSpec((tm, k), lambda i, j: (i, 0))]
    args = [x]
    if has_norm:
        in_specs.append(pl.BlockSpec((1, k), lambda i, j: (0, 0)))
        args.append(gain.reshape(1, k).astype(F32))
    in_specs.append(pl.BlockSpec((k, tn), lambda i, j: (0, j)))
    args.append(w)
    in_specs += list(extra_specs)
    args += list(extras)
    scratch = [pltpu.VMEM((tm, k), BF16)] if has_norm else []
    return pl.pallas_call(
        functools.partial(_mm_kernel, has_norm=has_norm, n_extra=len(extras), epilogue=epilogue),
        grid=(m // tm, n // tn),
        in_specs=in_specs,
        out_specs=out_specs,
        out_shape=out_shapes,
        scratch_shapes=scratch,
        compiler_params=_params("parallel", "arbitrary"),
        name=name,
    )(*args)


def _ret_in_epilogue(j, product, extra, outs, *, tn, qk_w, v_w):
    cos_ref, sin_ref = extra
    (o_ref,) = outs
    n_qk = (2 * qk_w) // tn
    n_v = v_w // tn

    @pl.when(j < n_qk)
    def _():
        acc = product()
        cos = cos_ref[...]
        sin = sin_ref[...]
        scale = jnp.where(j >= qk_w // tn, 0.0625, 1.0).astype(F32)
        for h in range(tn // 256):
            x1 = acc[:, h * 256:h * 256 + 128]
            x2 = acc[:, h * 256 + 128:(h + 1) * 256]
            o_ref[:, h * 256:h * 256 + 128] = ((x1 * cos - x2 * sin) * scale).astype(o_ref.dtype)
            o_ref[:, h * 256 + 128:(h + 1) * 256] = ((x2 * cos + x1 * sin) * scale).astype(o_ref.dtype)

    @pl.when(jnp.logical_and(j >= n_qk, j < n_qk + n_v))
    def _():
        o_ref[...] = product().astype(o_ref.dtype)

    @pl.when(j >= n_qk + n_v)
    def _():
        acc = product()
        o_ref[...] = (acc * jax.nn.sigmoid(acc)).astype(o_ref.dtype)


def _ret_in_proj(x, gain, w, cos, sin, *, qk_w, v_w, seq, out_dtype):
    m = x.shape[0]
    n = w.shape[1]
    tm = _tile(min(m, seq), 1024)
    tn = _tile(qk_w, 1024)
    nseq = seq // tm
    tab = pl.BlockSpec((tm, LANES), lambda i, j: (i % nseq, 0))
    (out,) = _matmul(
        x, w, name="ret_in_proj", gain=gain, extras=(cos, sin), extra_specs=(tab, tab),
        epilogue=functools.partial(_ret_in_epilogue, tn=tn, qk_w=qk_w, v_w=v_w),
        out_shapes=[jax.ShapeDtypeStruct((m, n), out_dtype)],
        out_specs=[pl.BlockSpec((tm, tn), lambda i, j: (i, j))],
        tm=tm, tn=tn)
    return out


def _ret_log_decay(heads):
    return jnp.log1p(-jnp.exp2(-5.0 - jnp.arange(heads, dtype=F32)))


def _group_norm_gate(o, gw, gb, gate):
    mu = jnp.mean(o, axis=-1, keepdims=True)
    d = o - mu
    var = jnp.mean(d * d, axis=-1, keepdims=True)
    return (d * lax.rsqrt(var + GN_EPS) * gw + gb) * gate


def _ret_chunk_kernel(h_ref, dec_ref, rs_ref, we_ref, gc_ref, gw_ref, gb_ref, y_ref, st_ref, r_ref,
                      *, heads, dk, dv):
    c = pl.program_id(1)
    qk_w = heads * dk
    v_w = heads * dv

    @pl.when(c == 0)
    def _():
        r_ref[...] = jnp.zeros_like(r_ref)

    for h in range(heads):
        q = h_ref[:, h * dk:(h + 1) * dk]
        k = h_ref[:, qk_w + h * dk:qk_w + (h + 1) * dk]
        v = h_ref[:, 2 * qk_w + h * dv:2 * qk_w + (h + 1) * dv]
        g = h_ref[:, 2 * qk_w + v_w + h * dv:2 * qk_w + v_w + (h + 1) * dv]
        r = r_ref[h]
        s = (_nt_dot(q, k) * dec_ref[h]).astype(BF16)
        rs = jnp.concatenate([rs_ref[h]] * (dv // LANES), axis=1)
        o = jnp.dot(s, v, preferred_element_type=F32)
        o = o + jnp.dot(q, r.astype(BF16), preferred_element_type=F32) * rs
        we = jnp.concatenate([we_ref[h]] * (dk // LANES), axis=1)
        kw = (k.astype(F32) * we).astype(BF16)
        gc = jnp.concatenate([gc_ref[h, 0:1, :]] * (dv // LANES), axis=1)
        r_ref[h] = r * gc + _tn_dot(kw, v)
        gw = gw_ref[:, h * dv:(h + 1) * dv]
        gb = gb_ref[:, h * dv:(h + 1) * dv]
        y_ref[:, h * dv:(h + 1) * dv] = _group_norm_gate(o, gw, gb, g.astype(F32)).astype(y_ref.dtype)

    @pl.when(c == pl.num_programs(1) - 1)
    def _():
        st_ref[0] = r_ref[...]


def _ret_prompt(hq, gn_w, gn_b, *, batch, seq, heads, dk, dv):
    chunk = RET_CHUNK
    nc = seq // chunk
    qk_w = heads * dk
    v_w = heads * dv
    log_g = _ret_log_decay(heads)
    n = jnp.arange(chunk, dtype=F32)
    diff = n[:, None] - n[None, :]
    causal = diff >= 0
    decay = jnp.where(causal[None], jnp.exp(jnp.where(causal, diff, 0.0)[None] * log_g[:, None, None]), 0.0)
    ones = jnp.ones((1, 1, LANES), F32)
    rowscale = jnp.exp((n + 1.0)[None, :] * log_g[:, None])[:, :, None] * ones
    w_end = jnp.exp((chunk - 1.0 - n)[None, :] * log_g[:, None])[:, :, None] * ones
    g_chunk = jnp.exp(chunk * log_g)[:, None, None] * jnp.ones((1, 8, LANES), F32)

    tab = lambda shape: pl.BlockSpec(shape, lambda b, c: (0, 0, 0))
    vec = pl.BlockSpec((1, v_w), lambda b, c: (0, 0))
    y, state = pl.pallas_call(
        functools.partial(_ret_chunk_kernel, heads=heads, dk=dk, dv=dv),
        grid=(batch, nc),
        in_specs=[
            pl.BlockSpec((chunk, hq.shape[1]), lambda b, c: (b * nc + c, 0)),
            tab((heads, chunk, chunk)), tab((heads, chunk, LANES)), tab((heads, chunk, LANES)),
            tab((heads, 8, LANES)), vec, vec,
        ],
        out_specs=[
            pl.BlockSpec((chunk, v_w), lambda b, c: (b * nc + c, 0)),
            pl.BlockSpec((1, heads, dk, dv), lambda b, c: (b, 0, 0, 0)),
        ],
        out_shape=[
            jax.ShapeDtypeStruct((batch * seq, v_w), BF16),
            jax.ShapeDtypeStruct((batch, heads, dk, dv), F32),
        ],
        scratch_shapes=[pltpu.VMEM((heads, dk, dv), F32)],
        compiler_params=_params("parallel", "arbitrary"),
        name="ret_chunk_scan",
    )(hq, decay, rowscale, w_end, g_chunk,
      gn_w.reshape(1, v_w).astype(F32), gn_b.reshape(1, v_w).astype(F32))
    return y, state


def _column(row):
    n = row.shape[1]
    eye = lax.broadcasted_iota(jnp.int32, (n, n), 0) == lax.broadcasted_iota(jnp.int32, (n, n), 1)
    return jnp.sum(jnp.where(eye, row, 0.0), axis=1, keepdims=True)


def _ret_sample_heads(gam_ref, head0, q_ref, k_ref, v_ref, g_ref, st_ref, gw_ref, gb_ref, y_ref, ns_ref,
                      *, hb, dk, dv):
    for h in range(hb):
        gam = gam_ref[head0 + h]
        q = q_ref[0, :, h * dk:(h + 1) * dk]
        k = k_ref[0, :, h * dk:(h + 1) * dk]
        v = v_ref[0, :, h * dv:(h + 1) * dv]
        g = g_ref[0, :, h * dv:(h + 1) * dv]
        r = st_ref[0, 0, h]
        s = jnp.sum(q * k, axis=-1, keepdims=True)
        o = s * v + jnp.sum(r * _column(q), axis=0, keepdims=True) * gam
        ns_ref[0, 0, h] = r * gam + _column(k) * v
        gw = gw_ref[:, h * dv:(h + 1) * dv]
        gb = gb_ref[:, h * dv:(h + 1) * dv]
        y_ref[0, :, h * dv:(h + 1) * dv] = _group_norm_gate(o, gw, gb, g).astype(y_ref.dtype)


def _ret_sample_io(unit, hq, state, gn_w, gn_b, *, layer, heads, hb, dk, dv):
    batch, n = hq.shape
    qk_w = heads * dk
    v_w = heads * dv

    def col(width, base):
        return pl.BlockSpec((1, 1, width), lambda *idx: (unit(*idx)[0], 0, base + unit(*idx)[1]))

    def st(lead):
        return pl.BlockSpec((1, 1, hb, dk, dv), lambda *idx: (lead, unit(*idx)[0], unit(*idx)[1], 0, 0))

    vec = pl.BlockSpec((1, hb * dv), lambda *idx: (0, unit(*idx)[1]))
    in_specs = [pl.BlockSpec(memory_space=pltpu.SMEM),
                col(hb * dk, 0), col(hb * dk, heads // hb),
                col(hb * dv, (2 * qk_w) // (hb * dv)), col(hb * dv, (2 * qk_w + v_w) // (hb * dv)),
                st(layer), vec, vec]
    out_specs = [pl.BlockSpec((1, 1, hb * dv), lambda *idx: (unit(*idx)[0], 0, unit(*idx)[1])), st(0)]
    hq3 = hq.reshape(batch, 1, n)
    args = [jnp.exp(1.0 * _ret_log_decay(heads)), hq3, hq3, hq3, hq3, state,
            gn_w.reshape(1, v_w).astype(F32), gn_b.reshape(1, v_w).astype(F32)]
    out_shapes = [jax.ShapeDtypeStruct((batch, 1, v_w), BF16),
                  jax.ShapeDtypeStruct((1,) + state.shape[1:], F32)]
    return in_specs, args, out_specs, out_shapes


def _ret_sample_kernel(gam_ref, *refs, heads, dk, dv):
    _ret_sample_heads(gam_ref, 0, *refs, hb=heads, dk=dk, dv=dv)


def _ret_sample(hq, state, gn_w, gn_b, *, layer, heads, dk, dv):
    batch = hq.shape[0]
    in_specs, args, out_specs, out_shapes = _ret_sample_io(
        lambda b: (b, 0), hq, state, gn_w, gn_b, layer=layer, heads=heads, hb=heads, dk=dk, dv=dv)
    y, new_state = pl.pallas_call(
        functools.partial(_ret_sample_kernel, heads=heads, dk=dk, dv=dv),
        grid=(batch,),
        in_specs=in_specs,
        out_specs=out_specs,
        out_shape=out_shapes,
        compiler_params=_params("parallel"),
        name="ret_sample_step",
    )(*args)
    return y.reshape(batch, heads * dv), new_state


def _residual_epilogue(j, product, extra, outs):
    (res_ref,) = extra
    (o_ref,) = outs
    o_ref[...] = res_ref[...] + product()


def _proj_residual(a, w, res):
    m = a.shape[0]
    n = w.shape[1]
    tm = _tile(m, 512)
    tn = _tile(n, 1024)
    blk = pl.BlockSpec((tm, tn), lambda i, j: (i, j))
    (out,) = _matmul(a, w, name="proj_residual", extras=(res,), extra_specs=(blk,), epilogue=_residual_epilogue,
                     out_shapes=[jax.ShapeDtypeStruct((m, n), F32)], out_specs=[blk], tm=tm, tn=tn)
    return out


RIDER_ARITY = {"ret": (8, 2, 0), "paged": (7, 1, 6)}


def _mlp_kernel(x_ref, g_ref, wu_ref, wd_ref, *rest, final_norm, rider):
    rest = list(rest)
    n_in, n_out, n_scr = RIDER_ARITY[rider[0]] if rider else (0, 0, 0)
    fg_ref = rest.pop(0) if final_norm else None
    ride_in = [rest.pop(0) for _ in range(n_in)]
    o_ref = rest.pop(0)
    ride_out = [rest.pop(0) for _ in range(n_out)]
    xn_ref = rest.pop(0)
    ride_scr = rest
    assert len(ride_scr) == n_scr
    f = pl.program_id(1)
    step = pl.program_id(0) * pl.num_programs(1) + f
    n_steps = pl.num_programs(0) * pl.num_programs(1)

    @pl.when(f == 0)
    def _():
        x = x_ref[...]
        ms = jnp.mean(x * x, axis=-1, keepdims=True)
        xn_ref[...] = (x * lax.rsqrt(ms + NORM_EPS) * g_ref[...]).astype(xn_ref.dtype)
        o_ref[...] = x

    if rider and rider[0] == "paged":
        _, kw, n_groups = rider
        pt_ref, ql_ref, qr_ref, cn_ref, kn_ref, ckv_hbm, kr_hbm = ride_in
        ckv_buf, kr_buf, sem, m_ref, l_ref, acc_ref = ride_scr
        _paged_prologue(step, n_steps, n_groups, pt_ref, ckv_hbm, kr_hbm, ckv_buf, kr_buf, sem, m_ref, l_ref, acc_ref,
                        layer=kw["layer"], sps=kw["sps"], pps=kw["pps"])

    h = jnp.dot(xn_ref[...], wu_ref[0], preferred_element_type=F32)
    if rider and rider[0] == "ret":
        _, hb, dk, dv, n_split = rider
        _ret_sample_heads(ride_in[0], lax.rem(step, n_split) * hb, *ride_in[1:], *ride_out, hb=hb, dk=dk, dv=dv)
    if rider and rider[0] == "paged":
        _paged_compute(step, ql_ref, qr_ref, ckv_buf, kr_buf, m_ref, l_ref, acc_ref,
                       sps=kw["sps"], pps=kw["pps"], rope_dim=kw["rope_dim"])
    h = jnp.square(jnp.maximum(h, 0.0)).astype(BF16)
    o_ref[...] += jnp.dot(h, wd_ref[0], preferred_element_type=F32)

    if rider and rider[0] == "paged":
        _paged_epilogue(step, n_groups, ql_ref, qr_ref, cn_ref, kn_ref, ride_out[0], m_ref, l_ref, acc_ref,
                        sps=kw["sps"], rope_dim=kw["rope_dim"])

    if final_norm:
        @pl.when(f == pl.num_programs(1) - 1)
        def _():
            y = o_ref[...]
            ms = jnp.mean(y * y, axis=-1, keepdims=True)
            o_ref[...] = y * lax.rsqrt(ms + NORM_EPS) * fg_ref[...]


def _mlp_residual(x, gain, w_up, w_down, layer, final_gain=None, ret_sample=None, paged=None):
    m, d = x.shape
    ff = w_up.shape[2]
    tm = _tile(m, 512)
    tf = _tile(ff, 1024)
    rider = None
    if paged is not None:
        sps, pps, n_seq_groups, n_groups = _paged_plan(paged["q_lat"], paged["page_table"])
        per_row, rem = divmod(n_seq_groups * n_groups, m // tm)
        if rem == 0 and per_row >= 1 and ff % per_row == 0 and (ff // per_row) % LANES == 0:
            tf = ff // per_row
            rider = ("paged", dict(layer=paged["layer"], sps=sps, pps=pps, rope_dim=paged["cache_kr_t"].shape[2]),
                     n_groups)
    grid = (m // tm, ff // tf)
    final_norm = final_gain is not None
    vec = pl.BlockSpec((1, d), lambda i, f: (0, 0))
    in_specs = [
        pl.BlockSpec((tm, d), lambda i, f: (i, 0)),
        vec,
        pl.BlockSpec((1, d, tf), lambda i, f: (layer, 0, f)),
        pl.BlockSpec((1, tf, d), lambda i, f: (layer, f, 0)),
    ]
    args = [x, gain.reshape(1, d).astype(F32), w_up, w_down]
    if final_norm:
        in_specs.append(vec)
        args.append(final_gain.reshape(1, d).astype(F32))
    out_specs = [pl.BlockSpec((tm, d), lambda i, f: (i, 0))]
    out_shapes = [jax.ShapeDtypeStruct((m, d), F32)]
    scratch = [pltpu.VMEM((tm, d), BF16)]
    if rider is not None:
        p = paged
        r_in, r_args, r_out, r_shapes, r_scratch = _paged_io(
            lambda i, f: (i * grid[1] + f) // n_groups, p["q_lat"], p["qr"], p["ckv_new"], p["kr_new"],
            p["cache_ckv"], p["cache_kr_t"], p["page_table"])
        in_specs += r_in
        args += r_args
        out_specs += r_out
        out_shapes += r_shapes
        scratch += r_scratch
    if ret_sample is not None:
        r = ret_sample
        batch = r["hq"].shape[0]
        n_split, rem = divmod(grid[0] * grid[1], batch)
        if rem == 0 and n_split >= 1 and r["heads"] % n_split == 0:
            hb = r["heads"] // n_split
            unit = lambda i, f: ((i * grid[1] + f) // n_split, (i * grid[1] + f) % n_split)
            r_in, r_args, r_out, r_shapes = _ret_sample_io(
                unit, r["hq"], r["state"], r["gn_w"], r["gn_b"],
                layer=r["layer"], heads=r["heads"], hb=hb, dk=r["dk"], dv=r["dv"])
            in_specs += r_in
            args += r_args
            out_specs += r_out
            out_shapes += r_shapes
            rider = ("ret", hb, r["dk"], r["dv"], n_split)
    outs = pl.pallas_call(
        functools.partial(_mlp_kernel, final_norm=final_norm, rider=rider),
        grid=grid,
        in_specs=in_specs,
        out_specs=out_specs,
        out_shape=out_shapes,
        scratch_shapes=scratch,
        compiler_params=_params("arbitrary" if rider else "parallel", "arbitrary"),
        name="mlp_residual_" + rider[0] if rider else "mlp_residual",
    )(*args)
    if paged is not None:
        return outs[0], (outs[1] if rider else None)
    if ret_sample is None:
        return outs[0]
    if rider is None:
        return outs[0], None, None
    return outs[0], outs[1].reshape(outs[1].shape[0], -1), outs[2]


def _mla_in_epilogue(j, product, extra, outs, *, q_lora, kv_lora):
    qg_ref, kg_ref, cos_ref, sa_ref, sb_ref = extra
    cq_ref, ckv_ref, ckvb_ref, kr_ref, krb_ref = outs
    acc = product()

    def norm(x, g):
        return x * lax.rsqrt(jnp.mean(x * x, axis=-1, keepdims=True) + NORM_EPS) * g

    cq_ref[...] = norm(acc[:, :q_lora], qg_ref[...]).astype(cq_ref.dtype)
    ckv = norm(acc[:, q_lora:q_lora + kv_lora], kg_ref[...])
    ckv_ref[...] = ckv
    ckvb_ref[...] = ckv.astype(ckvb_ref.dtype)
    kr = _rope64(acc[:, q_lora + kv_lora:], cos_ref[...], sa_ref[...], sb_ref[...])
    kr_ref[...] = kr
    krb_ref[...] = kr.astype(krb_ref.dtype)


def _mla_in_proj(x, gain, w, q_gain, kv_gain, tabs, *, q_lora, kv_lora, seq):
    m = x.shape[0]
    n = w.shape[1]
    tm = _tile(min(m, seq), 512)
    nseq = seq // tm
    tab = pl.BlockSpec((tm, LANES), lambda i, j: (i % nseq, 0))
    vec = lambda width: pl.BlockSpec((1, width), lambda i, j: (0, 0))
    blk = lambda width: pl.BlockSpec((tm, width), lambda i, j: (i, 0))
    return _matmul(
        x, w, name="mla_in_proj", gain=gain,
        extras=(q_gain.reshape(1, q_lora).astype(F32), kv_gain.reshape(1, kv_lora).astype(F32)) + tuple(tabs),
        extra_specs=(vec(q_lora), vec(kv_lora), tab, tab, tab),
        epilogue=functools.partial(_mla_in_epilogue, q_lora=q_lora, kv_lora=kv_lora),
        out_shapes=[jax.ShapeDtypeStruct((m, q_lora), BF16),
                    jax.ShapeDtypeStruct((m, kv_lora), F32),
                    jax.ShapeDtypeStruct((m, kv_lora), BF16),
                    jax.ShapeDtypeStruct((m, LANES), F32),
                    jax.ShapeDtypeStruct((m, LANES), BF16)],
        out_specs=[blk(q_lora), blk(kv_lora), blk(kv_lora), blk(LANES), blk(LANES)],
        tm=tm, tn=n)


def _mla_q_epilogue(j, product, extra, outs, *, n_nope, scale):
    cos_ref, sa_ref, sb_ref = extra
    (o_ref,) = outs

    @pl.when(j < n_nope)
    def _():
        o_ref[...] = (product() * scale).astype(o_ref.dtype)

    @pl.when(j >= n_nope)
    def _():
        acc = product()
        cos = cos_ref[...]
        sa = sa_ref[...]
        sb = sb_ref[...]
        for h in range(acc.shape[1] // LANES):
            x = acc[:, h * LANES:(h + 1) * LANES]
            o_ref[:, h * LANES:(h + 1) * LANES] = (_rope64(x, cos, sa, sb) * scale).astype(o_ref.dtype)


def _mla_q_proj(cq, w, tabs, *, nope_w, scale, seq, out_dtype):
    m = cq.shape[0]
    n = w.shape[1]
    tm = _tile(min(m, seq), 1024)
    tn = _tile(nope_w, 1024)
    nseq = seq // tm
    tab = pl.BlockSpec((tm, LANES), lambda i, j: (i % nseq, 0))
    (out,) = _matmul(
        cq, w, name="mla_q_proj", extras=tuple(tabs), extra_specs=(tab, tab, tab),
        epilogue=functools.partial(_mla_q_epilogue, n_nope=nope_w // tn, scale=scale),
        out_shapes=[jax.ShapeDtypeStruct((m, n), out_dtype)],
        out_specs=[pl.BlockSpec((tm, tn), lambda i, j: (i, j))],
        tm=tm, tn=tn)
    return out


def _cast_epilogue(j, product, extra, outs):
    (o_ref,) = outs
    o_ref[...] = product().astype(o_ref.dtype)


def _plain_matmul(a, w, out_dtype):
    m = a.shape[0]
    n = w.shape[1]
    tm = _tile(m, 1024)
    tn = _tile(n, 1024)
    (out,) = _matmul(a, w, name="mla_k_expand", epilogue=_cast_epilogue,
                     out_shapes=[jax.ShapeDtypeStruct((m, n), out_dtype)],
                     out_specs=[pl.BlockSpec((tm, tn), lambda i, j: (i, j))], tm=tm, tn=tn)
    return out


FLASH_TILE = 512
FLASH_HEADS_PER_STEP = 4


def _vt_kernel(x_ref, wt_ref, o_ref):
    o_ref[0] = _nt_dot(wt_ref[...], x_ref[...]).astype(o_ref.dtype)


def _mla_v_expand_t(ckvb, w_uv_t, tq):
    m, c = ckvb.shape
    n = w_uv_t.shape[0]
    tn = _tile(n, 1024)
    return pl.pallas_call(
        _vt_kernel,
        grid=(m // tq, n // tn),
        in_specs=[pl.BlockSpec((tq, c), lambda i, j: (i, 0)), pl.BlockSpec((tn, c), lambda i, j: (j, 0))],
        out_specs=pl.BlockSpec((1, tn, tq), lambda i, j: (i, j, 0)),
        out_shape=jax.ShapeDtypeStruct((m // tq, n, tq), BF16),
        compiler_params=_params("parallel", "arbitrary"),
        name="mla_v_expand_t",
    )(ckvb, w_uv_t)


def _flash_kernel(qn_ref, qr_ref, kn_ref, kr_ref, vt_ref, o_ref, *, tq, hps):
    qi = pl.program_id(2)
    dv = vt_ref.shape[1] // hps
    qs = [jnp.concatenate([qn_ref[:, h * LANES:(h + 1) * LANES], qr_ref[:, h * LANES:(h + 1) * LANES]], axis=1)
          for h in range(hps)]

    def scores(h, start):
        k = jnp.concatenate([kn_ref[pl.ds(start, tq), h * LANES:(h + 1) * LANES], kr_ref[pl.ds(start, tq), :]],
                            axis=1)
        return _nt_dot(k, qs[h])

    ones = jnp.ones((BF16_SUBLANES, tq), BF16)

    def probs(carry, s):
        m, _ = carry
        m_new = jnp.maximum(m, jnp.max(s, axis=0, keepdims=True))
        return m_new, jnp.exp2(m - m_new), jnp.exp2(s - m_new).astype(BF16)

    def update_all(carries, s_all, kb):
        st = [probs(carries[h], s_all[h]) for h in range(hps)]
        out = []
        for h in range(hps):
            m_new, corr, pb = st[h]
            vt = jnp.concatenate([vt_ref[kb, h * dv:(h + 1) * dv, :], ones], axis=0)
            out.append((m_new, carries[h][1] * corr + jnp.dot(vt, pb, preferred_element_type=F32)))
        return tuple(out)

    def body(kb, carries):
        start = pl.multiple_of(kb * tq, tq)
        return update_all(carries, [scores(h, start) for h in range(hps)], kb)

    init = tuple((jnp.full((1, tq), NEG, F32), jnp.zeros((dv + BF16_SUBLANES, tq), F32)) for _ in range(hps))
    carries = lax.fori_loop(0, qi, body, init)
    start = pl.multiple_of(qi * tq, tq)
    causal = lax.broadcasted_iota(jnp.int32, (tq, tq), 0) <= lax.broadcasted_iota(jnp.int32, (tq, tq), 1)
    final = update_all(carries, [jnp.where(causal, scores(h, start), NEG) for h in range(hps)], qi)
    for h in range(hps):
        acc = final[h][1]
        o_ref[:, h * dv:(h + 1) * dv] = (acc[:dv] / acc[dv:dv + 1]).T.astype(o_ref.dtype)


def _mla_prompt_attention(q, kn, krb, vt, *, batch, seq, heads):
    m = q.shape[0]
    tq = vt.shape[2]
    nq = seq // tq
    hps = FLASH_HEADS_PER_STEP if heads % FLASH_HEADS_PER_STEP == 0 else 1
    w = hps * LANES
    ng = heads // hps
    qblk = lambda off: pl.BlockSpec((tq, w), lambda b, h, i: (b * nq + i, off + h))
    return pl.pallas_call(
        functools.partial(_flash_kernel, tq=tq, hps=hps),
        grid=(batch, ng, nq),
        in_specs=[qblk(0), qblk(ng),
                  pl.BlockSpec((seq, w), lambda b, h, i: (b, h)),
                  pl.BlockSpec((seq, LANES), lambda b, h, i: (b, 0)),
                  pl.BlockSpec((nq, w, tq), lambda b, h, i: (b, h, 0))],
        out_specs=pl.BlockSpec((tq, w), lambda b, h, i: (b * nq + i, h)),
        out_shape=jax.ShapeDtypeStruct((m, heads * LANES), BF16),
        compiler_params=_params("parallel", "parallel", "arbitrary"),
        name="mla_prompt_flash",
    )(q, q, kn, krb, vt)


def _head_nt_kernel(a_ref, w_ref, o_ref):
    o_ref[0] = _nt_dot(a_ref[...], w_ref[...]).astype(o_ref.dtype)


def _absorb_uk(qn, w_uk2, *, heads):
    b = qn.shape[0]
    c = w_uk2.shape[0]
    return pl.pallas_call(
        _head_nt_kernel,
        grid=(heads,),
        in_specs=[pl.BlockSpec((b, LANES), lambda h: (0, h)), pl.BlockSpec((c, LANES), lambda h: (0, h))],
        out_specs=pl.BlockSpec((1, b, c), lambda h: (h, 0, 0)),
        out_shape=jax.ShapeDtypeStruct((heads, b, c), F32),
        compiler_params=_params("parallel"),
        name="mla_absorb_uk",
    )(qn, w_uk2)


def _head_nn_kernel(a_ref, w_ref, o_ref):
    o_ref[...] = jnp.dot(a_ref[0], w_ref[...], preferred_element_type=F32).astype(o_ref.dtype)


def _expand_uv(o_lat, w_uv2, *, heads):
    b = o_lat.shape[1]
    c = o_lat.shape[2]
    return pl.pallas_call(
        _head_nn_kernel,
        grid=(heads,),
        in_specs=[pl.BlockSpec((1, b, c), lambda h: (h, 0, 0)), pl.BlockSpec((c, LANES), lambda h: (0, h))],
        out_specs=pl.BlockSpec((b, LANES), lambda h: (0, h)),
        out_shape=jax.ShapeDtypeStruct((b, heads * LANES), BF16),
        compiler_params=_params("parallel"),
        name="mla_expand_uv",
    )(o_lat, w_uv2)


def _paged_prologue(step, n_steps, n_groups, pt_ref, ckv_hbm, kr_hbm, ckv_buf, kr_buf, sem, m_ref, l_ref, acc_ref,
                    *, layer, sps, pps):
    g = lax.rem(step, n_groups)
    slot = lax.rem(step, 2)

    def gather(t, buf_slot):
        row0 = (t // n_groups) * sps
        col0 = lax.rem(t, n_groups) * pps
        out = []
        for s in range(sps):
            for k in range(pps):
                pid = pt_ref[row0 + s, col0 + k]
                j = s * pps + k
                out.append(pltpu.make_async_copy(ckv_hbm.at[layer, pid], ckv_buf.at[buf_slot, j], sem.at[0, buf_slot]))
                out.append(pltpu.make_async_copy(kr_hbm.at[layer, pid], kr_buf.at[buf_slot, j], sem.at[1, buf_slot]))
        return out

    @pl.when(step == 0)
    def _():
        for c in gather(step, slot):
            c.start()

    @pl.when(step + 1 < n_steps)
    def _():
        for c in gather(step + 1, 1 - slot):
            c.start()

    for c in gather(step, slot):
        c.wait()

    @pl.when(g == 0)
    def _():
        m_ref[...] = jnp.full_like(m_ref, NEG)
        l_ref[...] = jnp.zeros_like(l_ref)
        acc_ref[...] = jnp.zeros_like(acc_ref)


def _paged_compute(step, ql_ref, qr_ref, ckv_buf, kr_buf, m_ref, l_ref, acc_ref, *, sps, pps, rope_dim):
    slot = lax.rem(step, 2)
    ql = [ql_ref[s] for s in range(sps)]
    qr = [qr_ref[s][:, :rope_dim] for s in range(sps)]
    pages = [[ckv_buf[slot, s * pps + i].astype(BF16) for i in range(pps)] for s in range(sps)]
    keys = [jnp.concatenate(pages[s], axis=0) for s in range(sps)]
    scores = []
    for s in range(sps):
        qlb = ql[s].astype(BF16)
        qrb = qr[s].astype(BF16)
        rope_keys = jnp.concatenate([kr_buf[slot, s * pps + i].astype(BF16) for i in range(pps)], axis=1)
        scores.append(_nt_dot(qlb, keys[s]) + jnp.dot(qrb, rope_keys, preferred_element_type=F32))
    stats = []
    for s in range(sps):
        m = m_ref[s]
        m_new = jnp.maximum(m, jnp.max(scores[s], axis=-1, keepdims=True))
        corr = jnp.exp(m - m_new)
        p = jnp.exp(scores[s] - m_new)
        l_new = l_ref[s] * corr + jnp.sum(p, axis=-1, keepdims=True)
        stats.append((m_new, corr, l_new, p.astype(BF16)))
    for s in range(sps):
        m_new, corr, l_new, pb = stats[s]
        acc_new = acc_ref[s] * corr + jnp.dot(pb, keys[s], preferred_element_type=F32)
        m_ref[s] = m_new
        l_ref[s] = l_new
        acc_ref[s] = acc_new


def _paged_epilogue(step, n_groups, ql_ref, qr_ref, cn_ref, kn_ref, o_ref, m_ref, l_ref, acc_ref, *, sps, rope_dim):
    @pl.when(lax.rem(step, n_groups) == n_groups - 1)
    def _():
        for s in range(sps):
            ql = ql_ref[s]
            m_new = m_ref[s]
            cn = cn_ref[s]
            s_new = (jnp.sum(ql * cn, axis=-1, keepdims=True)
                     + jnp.sum(qr_ref[s][:, :rope_dim] * kn_ref[s][:, :rope_dim], axis=-1, keepdims=True))
            m_fin = jnp.maximum(m_new, s_new)
            c_fin = jnp.exp(m_new - m_fin)
            p_new = jnp.exp(s_new - m_fin)
            l_fin = l_ref[s] * c_fin + p_new
            o_ref[s] = (acc_ref[s] * c_fin + p_new * cn) / l_fin


def _paged_kernel(pt_ref, ql_ref, qr_ref, cn_ref, kn_ref, ckv_hbm, kr_hbm, o_ref,
                  ckv_buf, kr_buf, sem, m_ref, l_ref, acc_ref, *, layer, sps, pps, rope_dim):
    n_groups = pl.num_programs(1)
    step = pl.program_id(0) * n_groups + pl.program_id(1)
    n_steps = pl.num_programs(0) * n_groups
    _paged_prologue(step, n_steps, n_groups, pt_ref, ckv_hbm, kr_hbm, ckv_buf, kr_buf, sem, m_ref, l_ref, acc_ref,
                    layer=layer, sps=sps, pps=pps)
    _paged_compute(step, ql_ref, qr_ref, ckv_buf, kr_buf, m_ref, l_ref, acc_ref, sps=sps, pps=pps, rope_dim=rope_dim)
    _paged_epilogue(step, n_groups, ql_ref, qr_ref, cn_ref, kn_ref, o_ref, m_ref, l_ref, acc_ref,
                    sps=sps, rope_dim=rope_dim)


PAGES_PER_STEP = 16
SEQS_PER_STEP = 2


def _paged_plan(q_lat, page_table):
    b = q_lat.shape[0]
    n_pages = page_table.shape[1]
    pps = _tile(n_pages, PAGES_PER_STEP)
    sps = SEQS_PER_STEP if b % SEQS_PER_STEP == 0 else 1
    return sps, pps, b // sps, n_pages // pps


def _paged_io(unit, q_lat, qr, ckv_new, kr_new, cache_ckv, cache_kr_t, page_table):
    b, heads, c = q_lat.shape
    page = cache_ckv.shape[2]
    rope_dim = cache_kr_t.shape[2]
    sps, pps, _, _ = _paged_plan(q_lat, page_table)
    per_b = lambda width: pl.BlockSpec((sps, heads, width), lambda *idx: (unit(*idx), 0, 0))
    new_b = lambda width: pl.BlockSpec((sps, 1, width), lambda *idx: (unit(*idx), 0, 0))
    hbm = pl.BlockSpec(memory_space=pl.ANY)
    in_specs = [pl.BlockSpec(memory_space=pltpu.SMEM), per_b(c), per_b(LANES), new_b(c), new_b(LANES), hbm, hbm]
    args = [page_table, q_lat, qr, ckv_new.reshape(b, 1, c), kr_new.reshape(b, 1, LANES), cache_ckv, cache_kr_t]
    out_specs = [per_b(c)]
    out_shapes = [jax.ShapeDtypeStruct((b, heads, c), F32)]
    n_slots = sps * pps
    scratch = [pltpu.VMEM((2, n_slots, page, c), cache_ckv.dtype),
               pltpu.VMEM((2, n_slots, rope_dim, page), cache_kr_t.dtype),
               pltpu.SemaphoreType.DMA((2, 2)),
               pltpu.VMEM((sps, heads, 1), F32), pltpu.VMEM((sps, heads, 1), F32),
               pltpu.VMEM((sps, heads, c), F32)]
    return in_specs, args, out_specs, out_shapes, scratch


def _mla_paged_attention(q_lat, qr, ckv_new, kr_new, cache_ckv, cache_kr_t, page_table, *, layer):
    sps, pps, n_seq_groups, n_groups = _paged_plan(q_lat, page_table)
    in_specs, args, out_specs, out_shapes, scratch = _paged_io(
        lambda i, g: i, q_lat, qr, ckv_new, kr_new, cache_ckv, cache_kr_t, page_table)
    (o_lat,) = pl.pallas_call(
        functools.partial(_paged_kernel, layer=layer, sps=sps, pps=pps, rope_dim=cache_kr_t.shape[2]),
        grid=(n_seq_groups, n_groups),
        in_specs=in_specs,
        out_specs=out_specs,
        out_shape=out_shapes,
        scratch_shapes=scratch,
        compiler_params=_params("arbitrary", "arbitrary"),
        name="mla_paged_attention",
    )(*args)
    return o_lat


def _concat_layers(parts):
    return parts[0] if len(parts) == 1 else jnp.concatenate(parts, axis=0)


def kernel(x_prompt, x_sample, state_ret, cache_kv_latent, cache_k_rope, page_table, norm_mix, norm_mlp, norm_final, w_ret_in, ret_gn_w, ret_gn_b, w_ret_out, w_mla_in, mla_q_norm, mla_kv_norm, w_mla_uq, w_mla_uk, w_mla_uv, w_mla_out, w_up, w_down):
    bp, seq, d = x_prompt.shape
    bs, ls, _ = x_sample.shape
    assert ls == 1, "the sample group is a single new token per sequence"
    depth = norm_mix.shape[0]
    assert depth >= 1, "the final norm is fused into the last layer's MLP"
    ret_heads, ret_dk, ret_dv = state_ret.shape[2:]
    qk_w = ret_heads * ret_dk
    v_w = ret_heads * ret_dv
    kv_lora, mla_heads, nope = w_mla_uk.shape[1:]
    v_dim = w_mla_uv.shape[3]
    q_lora = mla_q_norm.shape[1]
    rope_dim = cache_k_rope.shape[3]
    page = cache_kv_latent.shape[2]
    assert ret_dk == 2 * LANES and nope == LANES and v_dim == LANES and rope_dim * 2 == LANES
    past = page_table.shape[1] * page
    scale = float((nope + rope_dim) ** -0.5)

    pos_p = jnp.arange(seq, dtype=F32)
    pos_s = jnp.broadcast_to(past + jnp.arange(ls, dtype=F32), (bs,))
    ret_tab_p = _rope_cos_sin(pos_p, ret_dk // 2)
    ret_tab_s = _rope_cos_sin(pos_s, ret_dk // 2)
    mla_tab_p = _rope64_tables(pos_p)
    mla_tab_s = _rope64_tables(pos_s)

    xp = x_prompt.reshape(bp * seq, d)
    xs = x_sample.reshape(bs, d)
    w_up_b = w_up.astype(BF16)
    w_down_b = w_down.astype(BF16)
    ret_p, ret_s, ckv_p, kr_p, ckv_s, kr_s = [], [], [], [], [], []
    for i in range(depth):
        j = i // 2
        ride = None
        if i % 2 == 0:
            w_in = w_ret_in[j].astype(BF16)
            w_out = w_ret_out[j].astype(BF16)
            hq = _ret_in_proj(xp, norm_mix[i], w_in, *ret_tab_p, qk_w=qk_w, v_w=v_w, seq=seq, out_dtype=BF16)
            y, r_p = _ret_prompt(hq, ret_gn_w[j], ret_gn_b[j], batch=bp, seq=seq,
                                 heads=ret_heads, dk=ret_dk, dv=ret_dv)
            xp = _proj_residual(y, w_out, xp)
            hq = _ret_in_proj(xs, norm_mix[i], w_in, *ret_tab_s, qk_w=qk_w, v_w=v_w, seq=bs, out_dtype=F32)
            ride = dict(hq=hq, state=state_ret, gn_w=ret_gn_w[j], gn_b=ret_gn_b[j], layer=j,
                        heads=ret_heads, dk=ret_dk, dv=ret_dv)
            ret_p.append(r_p[None])
        else:
            w_in = jnp.pad(w_mla_in[j], ((0, 0), (0, LANES - rope_dim))).astype(BF16)
            w_uq = w_mla_uq[j].reshape(q_lora, mla_heads, nope + rope_dim)
            w_uq = jnp.concatenate(
                [w_uq[:, :, :nope].reshape(q_lora, mla_heads * nope),
                 jnp.pad(w_uq[:, :, nope:], ((0, 0), (0, 0), (0, LANES - rope_dim))).reshape(q_lora, mla_heads * LANES)],
                axis=1).astype(BF16)
            w_uk2 = w_mla_uk[j].reshape(kv_lora, mla_heads * nope).astype(BF16)
            w_uv2 = w_mla_uv[j].reshape(kv_lora, mla_heads * v_dim).astype(BF16)
            w_out = w_mla_out[j].astype(BF16)
            nope_w = mla_heads * nope

            cq, ckv, ckvb, kr, krb = _mla_in_proj(xp, norm_mix[i], w_in, mla_q_norm[j], mla_kv_norm[j], mla_tab_p,
                                                  q_lora=q_lora, kv_lora=kv_lora, seq=seq)
            q = _mla_q_proj(cq, w_uq, mla_tab_p, nope_w=nope_w, scale=scale * LOG2_E, seq=seq, out_dtype=BF16)
            kn = _plain_matmul(ckvb, w_uk2, BF16)
            vt = _mla_v_expand_t(ckvb, w_uv2.T, _tile(seq, FLASH_TILE))
            o = _mla_prompt_attention(q, kn, krb, vt, batch=bp, seq=seq, heads=mla_heads)
            xp = _proj_residual(o, w_out, xp)
            ckv_p.append(ckv.reshape(1, bp, seq // page, page, kv_lora))
            kr_p.append(kr[:, :rope_dim].reshape(1, bp, seq // page, page, rope_dim))

            cq, ckv, ckvb, kr, krb = _mla_in_proj(xs, norm_mix[i], w_in, mla_q_norm[j], mla_kv_norm[j], mla_tab_s,
                                                  q_lora=q_lora, kv_lora=kv_lora, seq=bs)
            q = _mla_q_proj(cq, w_uq, mla_tab_s, nope_w=nope_w, scale=scale, seq=bs, out_dtype=F32)
            q_lat = _absorb_uk(q[:, :nope_w].astype(BF16), w_uk2, heads=mla_heads)
            ride_paged = dict(q_lat=q_lat.transpose(1, 0, 2), qr=q[:, nope_w:].reshape(bs, mla_heads, LANES),
                              ckv_new=ckv, kr_new=kr, cache_ckv=cache_kv_latent,
                              cache_kr_t=jnp.swapaxes(cache_k_rope, 2, 3), page_table=page_table, layer=j)
            ckv_s.append(ckv.reshape(1, bs, ls, kv_lora))
            kr_s.append(kr[:, :rope_dim].reshape(1, bs, ls, rope_dim))
        final_gain = norm_final if i == depth - 1 else None
        if i % 2 == 1:
            xp, o_lat = _mlp_residual(xp, norm_mlp[i], w_up_b, w_down_b, i, final_gain, paged=ride_paged)
            if o_lat is None:
                p = ride_paged
                o_lat = _mla_paged_attention(p["q_lat"], p["qr"], p["ckv_new"], p["kr_new"], p["cache_ckv"],
                                             p["cache_kr_t"], p["page_table"], layer=j)
            o = _expand_uv(o_lat.transpose(1, 0, 2).astype(BF16), w_uv2, heads=mla_heads)
            xs = _proj_residual(o, w_out, xs)
        elif ride is None:
            xp = _mlp_residual(xp, norm_mlp[i], w_up_b, w_down_b, i, final_gain)
        else:
            xp, y, r_s = _mlp_residual(xp, norm_mlp[i], w_up_b, w_down_b, i, final_gain, ret_sample=ride)
            if y is None:
                y, r_s = _ret_sample(ride["hq"], state_ret, ride["gn_w"], ride["gn_b"], layer=j,
                                     heads=ret_heads, dk=ret_dk, dv=ret_dv)
            xs = _proj_residual(y, w_out, xs)
            ret_s.append(r_s)
        xs = _mlp_residual(xs, norm_mlp[i], w_up_b, w_down_b, i, final_gain)
    y_prompt = xp.reshape(bp, seq, d)
    y_sample = xs.reshape(bs, ls, d)
    return (y_prompt, y_sample) + tuple(_concat_layers(t) for t in (ret_p, ret_s, ckv_p, kr_p, ckv_s, kr_s))
```
